```python
import math
import jax, jax.numpy as jnp
from jax import lax
import numpy as np

D_MODEL = 4096
BATCH = 4
SEQ = 2048
DEPTH = 2
DEC_BATCH = 8
DEC_SEQ = 4
PAST_LEN = 16384
PAGE_SIZE = 128

N_SSM_LAYERS = (DEPTH + 1) // 2
N_ATTN_LAYERS = DEPTH // 2
D_SSM = D_MODEL
SSM_GROUP = 16
SSM_GROUPS = D_SSM // SSM_GROUP
SSM_STATE = 64
SSM_CHUNK = 128
DT_MIN = 0.001
DT_MAX = 0.1
HEAD_HALF = 64
HEAD_V = 2 * HEAD_HALF
N_HEADS = D_MODEL // HEAD_V
ATTN_SCALE = HEAD_HALF ** -0.5
Q_BLOCK = 128
SUBLN_EPS = 1e-5
NUM_BUCKETS = 32
MAX_DISTANCE = 128
N_MEM = 256
MEM_HEADS = 4
MEM_HEAD_DIM = 128
MEM_WIDTH = MEM_HEADS * MEM_HEAD_DIM
MEM_SCALE = MEM_HEAD_DIM ** -0.5
D_FF = 11008
CONV_WIDTH = 3
RMS_EPS = 1e-6

kernel_name = 'hybrid_s5_diffattn_convffn_step'


def rmsnorm(x, g, eps=RMS_EPS):
    xf = x.astype(jnp.float32)
    y = xf * lax.rsqrt(jnp.mean(xf * xf, axis=-1, keepdims=True) + eps) * g.astype(jnp.float32)
    return y.astype(x.dtype)


def rel_bucket(dist):
    n = jnp.maximum(dist, 0)
    max_exact = NUM_BUCKETS // 2
    nf = jnp.maximum(n, 1).astype(jnp.float32)
    large = max_exact + (jnp.log(nf / max_exact) / math.log(MAX_DISTANCE / max_exact)
                         * (NUM_BUCKETS - max_exact)).astype(jnp.int32)
    large = jnp.minimum(large, NUM_BUCKETS - 1)
    return jnp.where(n < max_exact, n, large)


def rel_bias_hqk(rel_bias, qpos, kpos):
    b = rel_bias[rel_bucket(qpos[:, None] - kpos[None, :])]
    return jnp.transpose(b, (2, 0, 1)).astype(jnp.float32)


def ssm_discretise(a_re, a_im, log_dt, b_re, b_im):
    lam = lax.complex(a_re.astype(jnp.float32), a_im.astype(jnp.float32))
    dt = jnp.exp(log_dt.astype(jnp.float32))[:, None]
    lam_bar = jnp.exp(lam * dt)
    b = lax.complex(b_re.astype(jnp.float32), b_im.astype(jnp.float32))
    b_bar = ((lam_bar - 1.0) / lam)[..., None] * b
    return lam_bar, b_bar


def _ssm_combine(e1, e2):
    a1, b1 = e1
    a2, b2 = e2
    return a1 * a2, a2 * b1 + b2


def ssm_block(h0, u, lam_bar, b_bar, c_re, c_im):
    bu = lax.complex(jnp.einsum('btgi,gpi->btgp', u, jnp.real(b_bar)),
                     jnp.einsum('btgi,gpi->btgp', u, jnp.imag(b_bar)))
    a = jnp.broadcast_to(lam_bar, bu.shape)
    a_cum, b_cum = lax.associative_scan(_ssm_combine, (a, bu), axis=1)
    hs = a_cum * h0[:, None] + b_cum
    y = (jnp.einsum('btgp,gip->btgi', jnp.real(hs), c_re)
         - jnp.einsum('btgp,gip->btgi', jnp.imag(hs), c_im))
    return hs[:, -1], y


def ssm_mixer(xn, h0, w_in, a_re, a_im, log_dt, b_re, b_im, c_re, c_im, d, w_glu):
    B, T, _ = xn.shape
    u = (xn @ w_in).astype(jnp.float32)
    ug = u.reshape(B, T, SSM_GROUPS, SSM_GROUP)
    lam_bar, b_bar = ssm_discretise(a_re, a_im, log_dt, b_re, b_im)
    cr = c_re.astype(jnp.float32)
    ci = c_im.astype(jnp.float32)
    if T % SSM_CHUNK == 0:
        nc = T // SSM_CHUNK
        uc = jnp.moveaxis(ug.reshape(B, nc, SSM_CHUNK, SSM_GROUPS, SSM_GROUP), 1, 0)

        def step(h, u_blk):
            return ssm_block(h, u_blk, lam_bar, b_bar, cr, ci)

        h_last, ys = lax.scan(step, h0, uc)
        y = jnp.moveaxis(ys, 0, 1).reshape(B, T, D_SSM)
    else:
        h_last, y = ssm_block(h0, ug, lam_bar, b_bar, cr, ci)
        y = y.reshape(B, T, D_SSM)
    y = y + d.astype(jnp.float32) * u
    z = jax.nn.gelu(y).astype(xn.dtype)
    g = z @ w_glu
    return g[..., :D_MODEL] * jax.nn.sigmoid(g[..., D_MODEL:]), h_last


def diff_qkv(xn, w_qkv):
    B, T, _ = xn.shape
    qkv = xn @ w_qkv
    w = N_HEADS * HEAD_V
    q = qkv[..., :w].reshape(B, T, N_HEADS, 2, HEAD_HALF)
    k = qkv[..., w:2 * w].reshape(B, T, N_HEADS, HEAD_V)
    v = qkv[..., 2 * w:].reshape(B, T, N_HEADS, HEAD_V)
    return q, k, v


def diff_lambda(lam_params, lam_init):
    lp = lam_params.astype(jnp.float32)
    return jnp.exp(jnp.sum(lp[0] * lp[1])) - jnp.exp(jnp.sum(lp[2] * lp[3])) + lam_init


def diff_merge(o, lam, lam_init, subln):
    d = o[:, 0] - lam * o[:, 1]
    d = d * lax.rsqrt(jnp.mean(d * d, axis=-1, keepdims=True) + SUBLN_EPS) * subln.astype(jnp.float32)
    d = d * (1.0 - lam_init)
    B, H, T, E = d.shape
    return jnp.transpose(d, (0, 2, 1, 3)).reshape(B, T, H * E)


def diff_attn_prompt(q, k, v, rel_bias, lam, lam_init, subln):
    B, S = q.shape[0], q.shape[1]
    qf = q.astype(jnp.float32) * ATTN_SCALE
    kf = k.astype(jnp.float32).reshape(B, S, N_HEADS, 2, HEAD_HALF)
    vf = v.astype(jnp.float32)
    kpos = jnp.arange(S)

    def block(i):
        q_blk = lax.dynamic_slice_in_dim(qf, i * Q_BLOCK, Q_BLOCK, axis=1)
        qpos = i * Q_BLOCK + jnp.arange(Q_BLOCK)
        s = jnp.einsum('bqhcd,bkhcd->bchqk', q_blk, kf) + rel_bias_hqk(rel_bias, qpos, kpos)
        s = jnp.where(qpos[:, None] >= kpos[None, :], s, -jnp.inf)
        p = jax.nn.softmax(s, axis=-1)
        o = jnp.einsum('bchqk,bkhe->bchqe', p, vf)
        return diff_merge(o, lam, lam_init, subln)

    outs = lax.map(block, jnp.arange(S // Q_BLOCK))
    return jnp.moveaxis(outs, 0, 1).reshape(B, S, N_HEADS * HEAD_V)


def _online_update(carry, s, vals):
    m, l, acc = carry
    m_new = jnp.maximum(m, jnp.max(s, axis=-1))
    corr = jnp.exp(m - m_new)
    p = jnp.exp(s - m_new[..., None])
    l = l * corr + jnp.sum(p, axis=-1)
    acc = acc * corr[..., None] + jnp.einsum('bchqk,bkhe->bchqe', p, vals)
    return m_new, l, acc


def diff_attn_sample(q, k_new, v_new, cache_k, cache_v, layer, page_table, rel_bias, lam, lam_init, subln):
    Bd, T = q.shape[0], q.shape[1]
    n_pages = page_table.shape[1]
    page = cache_k.shape[2]
    past = n_pages * page
    qf = q.astype(jnp.float32) * ATTN_SCALE
    qpos = past + jnp.arange(T)

    def page_step(carry, p_idx):
        phys = page_table[:, p_idx]
        kp = cache_k[layer, phys].astype(jnp.float32).reshape(Bd, page, N_HEADS, 2, HEAD_HALF)
        vp = cache_v[layer, phys].astype(jnp.float32)
        kpos = p_idx * page + jnp.arange(page)
        s = jnp.einsum('bqhcd,bkhcd->bchqk', qf, kp) + rel_bias_hqk(rel_bias, qpos, kpos)
        return _online_update(carry, s, vp), None

    init = (jnp.full((Bd, 2, N_HEADS, T), -1e30, jnp.float32),
            jnp.zeros((Bd, 2, N_HEADS, T), jnp.float32),
            jnp.zeros((Bd, 2, N_HEADS, T, HEAD_V), jnp.float32))
    carry, _ = lax.scan(page_step, init, jnp.arange(n_pages))
    kn = k_new.astype(jnp.float32).reshape(Bd, T, N_HEADS, 2, HEAD_HALF)
    s = jnp.einsum('bqhcd,bkhcd->bchqk', qf, kn) + rel_bias_hqk(rel_bias, qpos, qpos)
    s = jnp.where(qpos[:, None] >= qpos[None, :], s, -jnp.inf)
    m, l, acc = _online_update(carry, s, v_new.astype(jnp.float32))
    return diff_merge(acc / l[..., None], lam, lam_init, subln)


def mem_kv(mem, w_kv):
    B, M, _ = mem.shape
    kv = mem @ w_kv
    return (kv[..., :MEM_WIDTH].reshape(B, M, MEM_HEADS, MEM_HEAD_DIM),
            kv[..., MEM_WIDTH:].reshape(B, M, MEM_HEADS, MEM_HEAD_DIM))


def cross_attn(xn, mk, mv, w_q, w_o):
    B, T, _ = xn.shape
    q = (xn @ w_q).reshape(B, T, MEM_HEADS, MEM_HEAD_DIM).astype(jnp.float32)
    s = jnp.einsum('bthd,bmhd->bhtm', q, mk.astype(jnp.float32)) * MEM_SCALE
    p = jax.nn.softmax(s, axis=-1)
    o = jnp.einsum('bhtm,bmhd->bthd', p, mv.astype(jnp.float32)).reshape(B, T, MEM_WIDTH)
    return o.astype(xn.dtype) @ w_o


def conv_ffn(xn, buf, w_up, conv_w, conv_b, w_down):
    up = xn @ w_up
    T = up.shape[1]
    hp = jnp.concatenate([buf.astype(up.dtype), up], axis=1)
    c = conv_b
    for j in range(CONV_WIDTH):
        c = c + conv_w[j] * hp[:, j:j + T]
    h = jax.nn.silu(c[..., :D_FF]) * c[..., D_FF:]
    return h @ w_down, hp[:, -(CONV_WIDTH - 1):]


def setup_inputs(seed: int = 0) -> dict:
    key = jax.random.key(seed)
    ks = list(jax.random.split(key, 48))
    f32 = jnp.float32

    def nrm(i, shape, scale):
        return jax.random.normal(ks[i], shape, f32) * scale

    n_pages = PAST_LEN // PAGE_SIZE
    n_used = DEC_BATCH * n_pages
    n_pool = n_used + (n_used + 3) // 4
    perm = jax.random.permutation(ks[0], n_pool)
    page_table = perm[:n_used].reshape(DEC_BATCH, n_pages).astype(jnp.int32)

    a_im0 = jnp.broadcast_to(math.pi * jnp.arange(SSM_STATE, dtype=f32), (N_SSM_LAYERS, SSM_GROUPS, SSM_STATE))
    log_dt = jax.random.uniform(ks[1], (N_SSM_LAYERS, SSM_GROUPS), f32, math.log(DT_MIN), math.log(DT_MAX))

    return {
        'x_prompt': nrm(2, (BATCH, SEQ, D_MODEL), 1.0),
        'x_sample': nrm(3, (DEC_BATCH, DEC_SEQ, D_MODEL), 1.0),
        'cache_attn_k': nrm(4, (N_ATTN_LAYERS, n_pool, PAGE_SIZE, N_HEADS, HEAD_V), 1.0),
        'cache_attn_v': nrm(5, (N_ATTN_LAYERS, n_pool, PAGE_SIZE, N_HEADS, HEAD_V), 1.0),
        'cache_mem_k': nrm(6, (DEPTH, DEC_BATCH, N_MEM, MEM_HEADS, MEM_HEAD_DIM), 1.0),
        'cache_mem_v': nrm(7, (DEPTH, DEC_BATCH, N_MEM, MEM_HEADS, MEM_HEAD_DIM), 1.0),
        'state_ssm_re': nrm(8, (N_SSM_LAYERS, DEC_BATCH, SSM_GROUPS, SSM_STATE), 0.1),
        'state_ssm_im': nrm(9, (N_SSM_LAYERS, DEC_BATCH, SSM_GROUPS, SSM_STATE), 0.1),
        'state_ffn_conv': nrm(10, (DEPTH, DEC_BATCH, CONV_WIDTH - 1, 2 * D_FF), 1.0),
        'page_table': page_table,
        'mem_prompt': nrm(11, (BATCH, N_MEM, D_MODEL), 1.0),
        'norm_mix': 1.0 + nrm(12, (DEPTH, D_MODEL), 0.01),
        'norm_cross': 1.0 + nrm(13, (DEPTH, D_MODEL), 0.01),
        'norm_ffn': 1.0 + nrm(14, (DEPTH, D_MODEL), 0.01),
        'norm_final': 1.0 + nrm(15, (D_MODEL,), 0.01),
        'ssm_w_in': nrm(16, (N_SSM_LAYERS, D_MODEL, D_SSM), D_MODEL ** -0.5),
        'ssm_a_re': -0.5 + nrm(17, (N_SSM_LAYERS, SSM_GROUPS, SSM_STATE), 0.01),
        'ssm_a_im': a_im0 + nrm(18, (N_SSM_LAYERS, SSM_GROUPS, SSM_STATE), 0.01),
        'ssm_log_dt': log_dt,
        'ssm_b_re': nrm(19, (N_SSM_LAYERS, SSM_GROUPS, SSM_STATE, SSM_GROUP), (2 * SSM_GROUP) ** -0.5),
        'ssm_b_im': nrm(20, (N_SSM_LAYERS, SSM_GROUPS, SSM_STATE, SSM_GROUP), (2 * SSM_GROUP) ** -0.5),
        'ssm_c_re': nrm(21, (N_SSM_LAYERS, SSM_GROUPS, SSM_GROUP, SSM_STATE), (2 * SSM_STATE) ** -0.5),
        'ssm_c_im': nrm(22, (N_SSM_LAYERS, SSM_GROUPS, SSM_GROUP, SSM_STATE), (2 * SSM_STATE) ** -0.5),
        'ssm_d': nrm(23, (N_SSM_LAYERS, D_SSM), 1.0),
        'ssm_w_glu': nrm(24, (N_SSM_LAYERS, D_SSM, 2 * D_MODEL), D_SSM ** -0.5),
        'attn_w_qkv': nrm(25, (N_ATTN_LAYERS, D_MODEL, 3 * N_HEADS * HEAD_V), D_MODEL ** -0.5),
        'attn_lambda': nrm(26, (N_ATTN_LAYERS, 4, HEAD_HALF), 0.1),
        'attn_subln': 1.0 + nrm(27, (N_ATTN_LAYERS, HEAD_V), 0.01),
        'attn_w_o': nrm(28, (N_ATTN_LAYERS, N_HEADS * HEAD_V, D_MODEL), (N_HEADS * HEAD_V) ** -0.5),
        'rel_bias': nrm(29, (NUM_BUCKETS, N_HEADS), 0.3),
        'cross_w_q': nrm(30, (DEPTH, D_MODEL, MEM_WIDTH), D_MODEL ** -0.5),
        'cross_w_kv': nrm(31, (DEPTH, D_MODEL, 2 * MEM_WIDTH), D_MODEL ** -0.5),
        'cross_w_o': nrm(32, (DEPTH, MEM_WIDTH, D_MODEL), MEM_WIDTH ** -0.5),
        'ffn_w_up': nrm(33, (DEPTH, D_MODEL, 2 * D_FF), D_MODEL ** -0.5),
        'ffn_conv_w': nrm(34, (DEPTH, CONV_WIDTH, 2 * D_FF), CONV_WIDTH ** -0.5),
        'ffn_conv_b': nrm(35, (DEPTH, 2 * D_FF), 0.01),
        'ffn_w_down': nrm(36, (DEPTH, D_FF, D_MODEL), D_FF ** -0.5),
    }


def reference(x_prompt, x_sample, cache_attn_k, cache_attn_v, cache_mem_k, cache_mem_v,
              state_ssm_re, state_ssm_im, state_ffn_conv, page_table, mem_prompt,
              norm_mix, norm_cross, norm_ffn, norm_final,
              ssm_w_in, ssm_a_re, ssm_a_im, ssm_log_dt, ssm_b_re, ssm_b_im, ssm_c_re, ssm_c_im, ssm_d, ssm_w_glu,
              attn_w_qkv, attn_lambda, attn_subln, attn_w_o, rel_bias,
              cross_w_q, cross_w_kv, cross_w_o,
              ffn_w_up, ffn_conv_w, ffn_conv_b, ffn_w_down):
    xp, xs = x_prompt, x_sample
    bp = xp.shape[0]
    ssm_re_p, ssm_im_p, ssm_re_s, ssm_im_s = [], [], [], []
    k_p, v_p, k_s, v_s = [], [], [], []
    mk_p, mv_p, conv_p, conv_s = [], [], [], []
    for i in range(DEPTH):
        j = i // 2
        hp = rmsnorm(xp, norm_mix[i])
        hs = rmsnorm(xs, norm_mix[i])
        if i % 2 == 0:
            args = (ssm_w_in[j], ssm_a_re[j], ssm_a_im[j], ssm_log_dt[j], ssm_b_re[j], ssm_b_im[j],
                    ssm_c_re[j], ssm_c_im[j], ssm_d[j], ssm_w_glu[j])
            h0p = jnp.zeros((bp, SSM_GROUPS, SSM_STATE), jnp.complex64)
            h0s = lax.complex(state_ssm_re[j].astype(jnp.float32), state_ssm_im[j].astype(jnp.float32))
            op, hlp = ssm_mixer(hp, h0p, *args)
            os_, hls = ssm_mixer(hs, h0s, *args)
            ssm_re_p.append(jnp.real(hlp))
            ssm_im_p.append(jnp.imag(hlp))
            ssm_re_s.append(jnp.real(hls))
            ssm_im_s.append(jnp.imag(hls))
        else:
            lam_init = 0.8 - 0.6 * math.exp(-0.3 * i)
            lam = diff_lambda(attn_lambda[j], lam_init)
            qp, kp_, vp_ = diff_qkv(hp, attn_w_qkv[j])
            qs, ks_, vs_ = diff_qkv(hs, attn_w_qkv[j])
            ap = diff_attn_prompt(qp, kp_, vp_, rel_bias, lam, lam_init, attn_subln[j])
            as_ = diff_attn_sample(qs, ks_, vs_, cache_attn_k, cache_attn_v, j, page_table,
                                   rel_bias, lam, lam_init, attn_subln[j])
            op = ap.astype(xp.dtype) @ attn_w_o[j]
            os_ = as_.astype(xs.dtype) @ attn_w_o[j]
            k_p.append(kp_)
            v_p.append(vp_)
            k_s.append(ks_)
            v_s.append(vs_)
        xp = xp + op
        xs = xs + os_
        hp = rmsnorm(xp, norm_cross[i])
        hs = rmsnorm(xs, norm_cross[i])
        mkp, mvp = mem_kv(mem_prompt, cross_w_kv[i])
        mk_p.append(mkp)
        mv_p.append(mvp)
        xp = xp + cross_attn(hp, mkp, mvp, cross_w_q[i], cross_w_o[i])
        xs = xs + cross_attn(hs, cache_mem_k[i], cache_mem_v[i], cross_w_q[i], cross_w_o[i])
        hp = rmsnorm(xp, norm_ffn[i])
        hs = rmsnorm(xs, norm_ffn[i])
        buf0 = jnp.zeros((bp, CONV_WIDTH - 1, 2 * D_FF), xp.dtype)
        fp, cbp = conv_ffn(hp, buf0, ffn_w_up[i], ffn_conv_w[i], ffn_conv_b[i], ffn_w_down[i])
        fs, cbs = conv_ffn(hs, state_ffn_conv[i], ffn_w_up[i], ffn_conv_w[i], ffn_conv_b[i], ffn_w_down[i])
        conv_p.append(cbp)
        conv_s.append(cbs)
        xp = xp + fp
        xs = xs + fs
    y_prompt = rmsnorm(xp, norm_final)
    y_sample = rmsnorm(xs, norm_final)
    return (y_prompt, y_sample,
            jnp.stack(ssm_re_p), jnp.stack(ssm_im_p), jnp.stack(k_p), jnp.stack(v_p),
            jnp.stack(mk_p), jnp.stack(mv_p), jnp.stack(conv_p),
            jnp.stack(ssm_re_s), jnp.stack(ssm_im_s), jnp.stack(k_s), jnp.stack(v_s),
            jnp.stack(conv_s))
```

```python
import functools
import math

import jax
import jax.numpy as jnp
import numpy as np
from jax import lax
from jax.experimental import pallas as pl
from jax.experimental.pallas import tpu as pltpu

F32 = jnp.float32
BF16 = jnp.bfloat16

SSM_GROUP = 16
MAX_DISTANCE = 128
RMS_EPS = 1e-6
SUBLN_EPS = 1e-5
CONV_WIDTH = 3

LANES = 128
SUBLANES = 8
VMEM_LIMIT_BYTES = 56 * 1024 * 1024

MASK_VALUE = -1e30
NT_DIMS = (((1,), (1,)), ((), ()))


def _cparams(*sem):
    return pltpu.CompilerParams(dimension_semantics=sem, vmem_limit_bytes=VMEM_LIMIT_BYTES)


def _pick(n, cands):
    for c in cands:
        if n % c == 0:
            return c
    return n


def _rmsnorm_kernel(x_ref, g_ref, o_ref):
    x = x_ref[...]
    ms = jnp.mean(x * x, axis=-1, keepdims=True)
    o_ref[...] = (x * lax.rsqrt(ms + RMS_EPS) * g_ref[...]).astype(o_ref.dtype)


def _rmsnorm(x, g, out_dtype):
    m, d = x.shape
    tm = _pick(m, (256,))
    return pl.pallas_call(
        _rmsnorm_kernel,
        out_shape=jax.ShapeDtypeStruct((m, d), out_dtype),
        grid=(m // tm,),
        in_specs=[pl.BlockSpec((tm, d), lambda i: (i, 0)),
                  pl.BlockSpec((1, d), lambda i: (0, 0))],
        out_specs=pl.BlockSpec((tm, d), lambda i: (i, 0)),
        compiler_params=_cparams("parallel"),
        name="rmsnorm",
    )(x, g.reshape(1, d))


def _mm_kernel(*refs, nk, scale, has_res):
    if has_res:
        x_ref, w_ref, r_ref, o_ref = refs[:4]
        rest = refs[4:]
    else:
        x_ref, w_ref, o_ref = refs[:3]
        r_ref = None
        rest = refs[3:]

    def finish(acc):
        if scale is not None:
            acc = acc * scale
        if r_ref is not None:
            acc = r_ref[...] + acc
        o_ref[...] = acc.astype(o_ref.dtype)

    part = jnp.dot(x_ref[...], w_ref[...], preferred_element_type=F32)
    if nk == 1:
        finish(part)
        return
    acc_ref, = rest
    k = pl.program_id(2)

    @pl.when(k == 0)
    def _():
        acc_ref[...] = part

    @pl.when(k > 0)
    def _():
        acc_ref[...] += part

    @pl.when(k == nk - 1)
    def _():
        finish(acc_ref[...])


def _mm_tiles(m, n, k):
    tm = _pick(m, (1024,))
    if m >= 1024:
        tn = _pick(n, (512, 256, 128))
    else:
        tn = _pick(n, (1024, 512, 256, 128))
    tk = k if k <= 4096 else _pick(k, (2816, 2048, 1024, 512, 256, 128))
    return tm, tn, tk


def _mm(x, w, *, n=None, col_off=0, out_dtype=F32, res=None, scale=None, name="mm"):
    m, k = x.shape
    n = w.shape[1] if n is None else n
    tm, tn, tk = _mm_tiles(m, n, k)
    nk = k // tk
    assert col_off % tn == 0 and m % tm == 0 and n % tn == 0 and k % tk == 0
    cb = col_off // tn
    in_specs = [pl.BlockSpec((tm, tk), lambda i, j, kk: (i, kk)),
                pl.BlockSpec((tk, tn), lambda i, j, kk: (kk, j + cb))]
    args = [x, w]
    if res is not None:
        in_specs.append(pl.BlockSpec((tm, tn), lambda i, j, kk: (i, j)))
        args.append(res)
    return pl.pallas_call(
        functools.partial(_mm_kernel, nk=nk, scale=scale, has_res=res is not None),
        out_shape=jax.ShapeDtypeStruct((m, n), out_dtype),
        grid=(m // tm, n // tn, nk),
        in_specs=in_specs,
        out_specs=pl.BlockSpec((tm, tn), lambda i, j, kk: (i, j)),
        scratch_shapes=[pltpu.VMEM((tm, tn), F32)] if nk > 1 else [],
        compiler_params=_cparams("parallel", "parallel", "arbitrary"),
        name=name,
    )(*args)


def _glu_kernel(x_ref, wa_ref, wb_ref, r_ref, o_ref):
    x = x_ref[...]
    a = jnp.dot(x, wa_ref[...], preferred_element_type=F32)
    b = jnp.dot(x, wb_ref[...], preferred_element_type=F32)
    o_ref[...] = r_ref[...] + a * jax.nn.sigmoid(b)


def _glu_mm(x, w, res):
    m, k = x.shape
    d = w.shape[1] // 2
    tm = _pick(m, (1024,))
    tn = _pick(d, (512, 256, 128))
    nj = d // tn
    return pl.pallas_call(
        _glu_kernel,
        out_shape=jax.ShapeDtypeStruct((m, d), F32),
        grid=(m // tm, nj),
        in_specs=[pl.BlockSpec((tm, k), lambda i, j: (i, 0)),
                  pl.BlockSpec((k, tn), lambda i, j: (0, j)),
                  pl.BlockSpec((k, tn), lambda i, j: (0, j + nj)),
                  pl.BlockSpec((tm, tn), lambda i, j: (i, j))],
        out_specs=pl.BlockSpec((tm, tn), lambda i, j: (i, j)),
        compiler_params=_cparams("parallel", "parallel"),
        name="glu_mm",
    )(x, w, w, res)


def _ssm_discretise(ar, ai, log_dt):
    dt = jnp.exp(log_dt)
    mag = jnp.exp(ar * dt)
    return mag * jnp.cos(ai * dt), mag * jnp.sin(ai * dt)


def _ssm_prep_kernel(are_ref, aim_ref, ldt_ref, arc_ref, aic_ref, ldc_ref, btr_ref, bti_ref, ctr_ref, cti_ref,
                     lr_ref, li_ref, wb_ref, wc_ref):
    lr_ref[...], li_ref[...] = _ssm_discretise(are_ref[...], aim_ref[...], ldt_ref[...])
    ar, ai = arc_ref[...], aic_ref[...]
    lr, li = _ssm_discretise(ar, ai, ldc_ref[...])
    den = ar * ar + ai * ai
    nr = lr - 1.0
    kr = (nr * ar + li * ai) / den
    ki = (li * ar - nr * ai) / den
    btr, bti = btr_ref[...], bti_ref[...]
    rows = SUBLANES * SSM_GROUP
    b_re = (kr * btr - ki * bti).reshape(rows, LANES)
    b_im = (kr * bti + ki * btr).reshape(rows, LANES)
    c_re = ctr_ref[...].reshape(rows, LANES)
    c_im = -cti_ref[...].reshape(rows, LANES)
    row_group = lax.broadcasted_iota(jnp.int32, (rows, LANES), 0) // SSM_GROUP
    lane_half = lax.broadcasted_iota(jnp.int32, (rows, LANES), 1) // (LANES // 2)
    nblk = SUBLANES // 2
    for half, (bsrc, csrc) in enumerate(((b_re, c_re), (b_im, c_im))):
        for mblk in range(nblk):
            keep = row_group == 2 * mblk + lane_half
            sl = slice((half * nblk + mblk) * LANES, (half * nblk + mblk + 1) * LANES)
            wb_ref[:, sl] = jnp.where(keep, bsrc, 0.0).astype(wb_ref.dtype)
            wc_ref[:, sl] = jnp.where(keep, csrc, 0.0).astype(wc_ref.dtype)


def _ssm_prep(a_re, a_im, log_dt, b_re, b_im, c_re, c_im):
    g, p = a_re.shape
    assert 2 * p == LANES and g % SUBLANES == 0
    nt = g // SUBLANES
    width = SUBLANES * LANES
    dup = lambda a: jnp.concatenate([a, a], axis=-1)
    btr = dup(jnp.swapaxes(b_re, 1, 2))
    bti = dup(jnp.swapaxes(b_im, 1, 2))
    ldt = jnp.broadcast_to(log_dt[:, None], (g, LANES))
    on_rows = lambda a: jnp.broadcast_to(a[:, None, :], (g, SSM_GROUP, LANES))
    vec =pl.BlockSpec((SUBLANES, LANES), lambda j: (j, 0))
    cube = pl.BlockSpec((SUBLANES, SSM_GROUP, LANES), lambda j: (j, 0, 0))
    wide = pl.BlockSpec((None, SUBLANES * SSM_GROUP, width), lambda j: (j, 0, 0))
    return pl.pallas_call(
        _ssm_prep_kernel,
        out_shape=[jax.ShapeDtypeStruct((g, LANES), F32), jax.ShapeDtypeStruct((g, LANES), F32),
                   jax.ShapeDtypeStruct((nt, SUBLANES * SSM_GROUP, width), BF16),
                   jax.ShapeDtypeStruct((nt, SUBLANES * SSM_GROUP, width), BF16)],
        grid=(nt,),
        in_specs=[vec, vec, vec, cube, cube, cube, cube, cube, cube, cube],
        out_specs=[vec, vec, wide, wide],
        compiler_params=_cparams("parallel"),
        name="ssm_prep",
    )(dup(a_re), dup(a_im), ldt, on_rows(dup(a_re)), on_rows(dup(a_im)), on_rows(ldt),
      btr, bti, dup(c_re), dup(c_im))


def _ssm_scan_kernel(*refs, paired, steps):
    if paired:
        (u_ref, wb0_ref, wb1_ref, wc0_ref, wc1_ref, lr_ref, li_ref, d_ref, h0_ref,
         z_ref, hf_ref, bu_ref, hs_ref) = refs
    else:
        (u_ref, wb0_ref, wc0_ref, lr_ref, li_ref, d_ref, h0_ref,
         z_ref, hf_ref, bu_ref, hs_ref) = refs
    half = hs_ref.shape[1] // 2

    @pl.when(pl.program_id(1) == 0)
    def _():
        hs_ref[...] = h0_ref[...]

    u = u_ref[...]
    ub = u.astype(BF16)
    bu = jnp.dot(ub, wb0_ref[...], preferred_element_type=F32)
    if paired:
        odd = lax.broadcasted_iota(jnp.int32, (u.shape[0], 1), 0) % 2 == 1
        bu = jnp.where(odd, jnp.dot(ub, wb1_ref[...], preferred_element_type=F32), bu)
    bu_ref[...] = bu

    lr, li = lr_ref[...], li_ref[...]

    def step(t, carry):
        hr, hi = carry
        row = pl.multiple_of(t * SUBLANES, SUBLANES)
        blk = bu_ref[pl.ds(row, SUBLANES), :]
        nr = hr * lr - hi * li + blk[:, :half]
        ni = hr * li + hi * lr + blk[:, half:]
        bu_ref[pl.ds(row, SUBLANES), :] = jnp.concatenate([nr, ni], axis=1)
        return nr, ni

    h0 = hs_ref[...]
    hr, hi = lax.fori_loop(0, steps, step, (h0[:, :half], h0[:, half:]))
    hfin = jnp.concatenate([hr, hi], axis=1)
    hs_ref[...] = hfin
    hf_ref[...] = hfin

    hs = bu_ref[...].astype(BF16)
    y = lax.dot_general(hs, wc0_ref[...], NT_DIMS, preferred_element_type=F32)
    if paired:
        y = jnp.where(odd, lax.dot_general(hs, wc1_ref[...], NT_DIMS, preferred_element_type=F32), y)
    y = (y.reshape(steps, SUBLANES, LANES) + d_ref[...][None] * u.reshape(steps, SUBLANES, LANES))
    z_ref[...] = jax.nn.gelu(y).reshape(steps * SUBLANES, LANES).astype(z_ref.dtype)


def _ssm_scan(u_rows, wb, wc, lr_slab, li_slab, d_slab, h0, *, paired):
    rows, c = u_rows.shape
    t = rows // SUBLANES
    nj = c // LANES
    lt = _pick(t, (64,))
    width = wb.shape[2]
    off = wb.shape[0] // 2
    u_spec = pl.BlockSpec((lt * SUBLANES, LANES), lambda j, s: (s, j))
    w_lo = pl.BlockSpec((None, LANES, width), lambda j, s: (j, 0, 0))
    w_hi = pl.BlockSpec((None, LANES, width), lambda j, s: (j + off, 0, 0))
    slab = lambda n: pl.BlockSpec((None, SUBLANES, n), lambda j, s: (j, 0, 0))
    if paired:
        in_specs = [u_spec, w_lo, w_hi, w_lo, w_hi]
        args = [u_rows, wb, wb, wc, wc]
    else:
        in_specs = [u_spec, w_lo, w_lo]
        args = [u_rows, wb, wc]
    in_specs += [slab(width // 2), slab(width // 2), slab(LANES), slab(width)]
    args += [lr_slab, li_slab, d_slab, h0]
    return pl.pallas_call(
        functools.partial(_ssm_scan_kernel, paired=paired, steps=lt),
        out_shape=[jax.ShapeDtypeStruct((rows, c), BF16),
                   jax.ShapeDtypeStruct((nj, SUBLANES, width), F32)],
        grid=(nj, t // lt),
        in_specs=in_specs,
        out_specs=[u_spec, slab(width)],
        scratch_shapes=[pltpu.VMEM((lt * SUBLANES, width), F32), pltpu.VMEM((SUBLANES, width), F32)],
        compiler_params=_cparams("parallel", "arbitrary"),
        name="ssm_scan",
    )(*args)


def _ssm_tiles(v, nt):
    return v[:, : LANES // 2].reshape(nt, SUBLANES * (LANES // 2))


def _ssm_mixer_prompt(u, prep, d, batch):
    lr, li, wb, wc = prep
    nt = wb.shape[0]
    bt, dm = u.shape
    t = bt // batch
    assert 2 * batch == SUBLANES and nt % 2 == 0
    hw = SUBLANES * (LANES // 2)
    u_rows = u.reshape(batch, t, 2, dm // 2).transpose(1, 0, 2, 3).reshape(t * SUBLANES, dm // 2)

    def pair_slab(tiles):
        n = tiles.shape[1]
        s = jnp.stack([tiles[: nt // 2], tiles[nt // 2:]], axis=1)
        return jnp.broadcast_to(s[:, None], (nt // 2, batch, 2, n)).reshape(nt // 2, SUBLANES, n)

    z_rows, hfin = _ssm_scan(u_rows, wb, wc, pair_slab(_ssm_tiles(lr, nt)), pair_slab(_ssm_tiles(li, nt)),
                             pair_slab(d.reshape(nt, LANES)), jnp.zeros((nt // 2, SUBLANES, 2 * hw), F32),
                             paired=True)
    z = z_rows.reshape(t, batch, 2, dm // 2).transpose(1, 0, 2, 3).reshape(bt, dm)

    def states(h):
        h = h.reshape(nt // 2, batch, 2, SUBLANES, LANES // 2).transpose(1, 2, 0, 3, 4)
        return h.reshape(batch, nt * SUBLANES, LANES // 2)

    return z, states(hfin[..., :hw]), states(hfin[..., hw:])


def _ssm_mixer_sample(u, prep, d, h0_re, h0_im):
    lr, li, wb, wc = prep
    nt = wb.shape[0]
    bd = h0_re.shape[0]
    bt, dm = u.shape
    t = bt // bd
    assert bd == SUBLANES
    hw = SUBLANES * (LANES // 2)
    u_rows = u.reshape(bd, t, dm).transpose(1, 0, 2).reshape(t * bd, dm)
    slab = lambda tiles: jnp.broadcast_to(tiles[:, None], (nt, SUBLANES, tiles.shape[1]))
    to_tiles = lambda h: h.reshape(bd, nt, hw).transpose(1, 0, 2)
    h0 = jnp.concatenate([to_tiles(h0_re), to_tiles(h0_im)], axis=-1)
    z_rows, hfin = _ssm_scan(u_rows, wb, wc, slab(_ssm_tiles(lr, nt)), slab(_ssm_tiles(li, nt)),
                             slab(d.reshape(nt, LANES)), h0, paired=False)
    z = z_rows.reshape(t, bd, dm).transpose(1, 0, 2).reshape(bt, dm)
    states = lambda h: h.transpose(1, 0, 2).reshape(bd, nt * SUBLANES, LANES // 2)
    return z, states(hfin[..., :hw]), states(hfin[..., hw:])


def _rel_bucket_np(dist, num_buckets):
    n = np.maximum(dist, 0)
    max_exact = num_buckets // 2
    nf = np.maximum(n, 1).astype(np.float32)
    large = max_exact + (np.log(nf / np.float32(max_exact)) / np.float32(math.log(MAX_DISTANCE / max_exact))
                         * np.float32(num_buckets - max_exact)).astype(np.int32)
    large = np.minimum(large, num_buckets - 1)
    return np.where(n < max_exact, n, large).astype(np.int32)


def _far_distance(num_buckets):
    b = _rel_bucket_np(np.arange(4 * MAX_DISTANCE), num_buckets)
    return int(np.max(np.nonzero(b != num_buckets - 1)[0])) + 1


def _bias_kernel(rb_ref, idx_ref, o_ref, *, num_buckets):
    h = pl.program_id(0)
    idx = idx_ref[...]
    acc = jnp.full(idx.shape, MASK_VALUE, F32)
    for b in range(num_buckets):
        acc = jnp.where(idx == b, rb_ref[b, h], acc)
    o_ref[...] = acc


def _bias_tables(rel_bias, idx):
    nb, nh = rel_bias.shape
    r, c = idx.shape
    return pl.pallas_call(
        functools.partial(_bias_kernel, num_buckets=nb),
        out_shape=jax.ShapeDtypeStruct((nh, r, c), F32),
        grid=(nh,),
        in_specs=[pl.BlockSpec(memory_space=pltpu.SMEM),
                  pl.BlockSpec((r, c), lambda h: (0, 0))],
        out_specs=pl.BlockSpec((None, r, c), lambda h: (h, 0, 0)),
        compiler_params=_cparams("parallel"),
        name="rel_bias_tables",
    )(rel_bias, jnp.asarray(idx))


def _diff_lambda(lam_ref, lam_init):
    lp = lam_ref[...]
    e1 = jnp.exp(jnp.sum(lp[0:1] * lp[1:2], keepdims=True))
    e2 = jnp.exp(jnp.sum(lp[2:3] * lp[3:4], keepdims=True))
    return e1 - e2 + lam_init


def _diff_merge(o1, o2, lam, subln, lam_init):
    d = o1 - lam * o2
    d = d * lax.rsqrt(jnp.mean(d * d, axis=-1, keepdims=True) + SUBLN_EPS) * subln
    return d * (1.0 - lam_init)


def _attn_prompt_kernel(lam_ref, subln_ref, q_ref, k_ref, v_ref, bias_ref, o_ref,
                        m_ref, l_ref, acc_ref, *, blk, lam_init):
    qi = pl.program_id(2)
    q = q_ref[...]
    lane = lax.broadcasted_iota(jnp.int32, q.shape, 1)
    hh = q.shape[1] // 2
    qq = jnp.concatenate([jnp.where(lane < hh, q, 0), jnp.where(lane >= hh, q, 0)], axis=0)

    m_ref[...] = jnp.full(m_ref.shape, MASK_VALUE, F32)
    l_ref[...] = jnp.zeros(l_ref.shape, F32)
    acc_ref[...] = jnp.zeros(acc_ref.shape, F32)

    def attend(kb, bias):
        start = pl.multiple_of(kb * blk, blk)
        k = k_ref[pl.ds(start, blk), :].astype(BF16)
        v = v_ref[pl.ds(start, blk), :].astype(BF16)
        s = lax.dot_general(qq, k, NT_DIMS, preferred_element_type=F32) + bias
        m_prev = m_ref[...]
        m_new = jnp.maximum(m_prev, jnp.max(s, axis=-1, keepdims=True))
        corr = jnp.exp(m_prev - m_new)
        p = jnp.exp(s - m_new)
        l_ref[...] = l_ref[...] * corr + jnp.sum(p, axis=-1, keepdims=True)
        acc_ref[...] = acc_ref[...] * corr + jnp.dot(p.astype(BF16), v, preferred_element_type=F32)
        m_ref[...] = m_new

    far = bias_ref[1, blk - 1:blk, 0:1]

    def far_body(kb, carry):
        attend(kb, far)
        return carry

    lax.fori_loop(0, jnp.maximum(qi - 1, 0), far_body, 0)

    @pl.when(qi >= 1)
    def _():
        b1 = bias_ref[1]
        attend(qi - 1, jnp.concatenate([b1, b1], axis=0))

    b0 = bias_ref[0]
    attend(qi, jnp.concatenate([b0, b0], axis=0))

    o = acc_ref[...] / l_ref[...]
    lam = _diff_lambda(lam_ref, lam_init)
    o_ref[...] = _diff_merge(o[:blk], o[blk:], lam, subln_ref[...], lam_init).astype(o_ref.dtype)


def _attn_prompt(q, k, v, bias, lam_p, subln, *, batch, lam_init, blk):
    bs, he = q.shape
    s = bs // batch
    e = LANES
    nh = he // e
    nq = s // blk
    return pl.pallas_call(
        functools.partial(_attn_prompt_kernel, blk=blk, lam_init=lam_init),
        out_shape=jax.ShapeDtypeStruct((bs, he), BF16),
        grid=(batch, nh, nq),
        in_specs=[pl.BlockSpec(lam_p.shape, lambda b, h, i: (0, 0)),
                  pl.BlockSpec((1, e), lambda b, h, i: (0, 0)),
                  pl.BlockSpec((blk, e), lambda b, h, i: (b * nq + i, h)),
                  pl.BlockSpec((s, e), lambda b, h, i: (b, h)),
                  pl.BlockSpec((s, e), lambda b, h, i: (b, h)),
                  pl.BlockSpec((None, 2, blk, blk), lambda b, h, i: (h, 0, 0, 0))],
        out_specs=pl.BlockSpec((blk, e), lambda b, h, i: (b * nq + i, h)),
        scratch_shapes=[pltpu.VMEM((2 * blk, 1), F32), pltpu.VMEM((2 * blk, 1), F32),
                        pltpu.VMEM((2 * blk, e), F32)],
        compiler_params=_cparams("parallel", "parallel", "arbitrary"),
        name="diff_attn_prompt",
    )(lam_p, subln.reshape(1, e), q, k, v, bias)


HEAD_GROUP = SUBLANES


def _attn_sample_kernel(pt_ref, lam_ref, subln_ref, q_ref, ck_ref, cv_ref, kn_ref, vn_ref,
                        bias_ref, biasn_ref, o_ref, m_ref, l_ref, acc_ref, *, n_pages, lam_init):
    p = pl.program_id(1)
    ng = q_ref.shape[0]

    @pl.when(p == 0)
    def _():
        m_ref[...] = jnp.full(m_ref.shape, MASK_VALUE, F32)
        l_ref[...] = jnp.zeros(l_ref.shape, F32)
        acc_ref[...] = jnp.zeros(acc_ref.shape, F32)

    def attend(k_ref, v_ref, b_ref):
        keys = k_ref.shape[0]
        for g in range(ng):
            hs = slice(g * HEAD_GROUP, (g + 1) * HEAD_GROUP)
            kg = k_ref[:, hs, :].reshape(keys * HEAD_GROUP, LANES).astype(BF16)
            vg = v_ref[:, hs, :].reshape(keys * HEAD_GROUP, LANES).astype(BF16)
            s = lax.dot_general(q_ref[g], kg, NT_DIMS, preferred_element_type=F32) + b_ref[g]
            m_prev = m_ref[g]
            m_new = jnp.maximum(m_prev, jnp.max(s, axis=-1, keepdims=True))
            corr = jnp.exp(m_prev - m_new)
            pe = jnp.exp(s - m_new)
            l_ref[g] = l_ref[g] * corr + jnp.sum(pe, axis=-1, keepdims=True)
            acc_ref[g] = acc_ref[g] * corr + jnp.dot(pe.astype(BF16), vg, preferred_element_type=F32)
            m_ref[g] = m_new

    @pl.when(p < n_pages)
    def _():
        attend(ck_ref, cv_ref, bias_ref)

    @pl.when(p == n_pages)
    def _():
        attend(kn_ref, vn_ref, biasn_ref)
        o = acc_ref[...] / l_ref[...]
        half = o.shape[1] // 2
        lam = _diff_lambda(lam_ref, lam_init)
        o_ref[...] = _diff_merge(o[:, :half], o[:, half:], lam, subln_ref[...], lam_init).astype(o_ref.dtype)


def _attn_sample(q, k_new, v_new, cache_k, cache_v, layer, page_table, tabs, lam_p, subln, *, lam_init):
    bd, n_pages = page_table.shape
    page, nh, e = cache_k.shape[2:]
    t = q.shape[0] // bd
    assert e == LANES and nh % HEAD_GROUP == 0 and t <= SUBLANES
    ng = nh // HEAD_GROUP
    hh = e // 2
    rows = 2 * HEAD_GROUP * t

    q5 = q.reshape(bd, t, ng, HEAD_GROUP, e).transpose(0, 2, 3, 1, 4)
    lane = jnp.arange(e) < hh
    qm = jnp.stack([jnp.where(lane, q5, 0), jnp.where(lane, 0, q5)], axis=2)
    qm = qm.reshape(bd, ng, rows, e)

    same = np.arange(HEAD_GROUP)[:, None] == np.arange(HEAD_GROUP)[None, :]

    def expand(tab, keys):
        tb = tab[:, :t, :keys].reshape(ng, HEAD_GROUP, t, keys)
        full = jnp.where(same[None, :, None, None, :], tb[..., None], MASK_VALUE)
        full = jnp.broadcast_to(full[:, None], (ng, 2, HEAD_GROUP, t, keys, HEAD_GROUP))
        return full.reshape(ng, rows, keys * HEAD_GROUP)

    bias_pages = jnp.stack([expand(tabs[:, 0], page), expand(tabs[:, 1], page)])
    bias_new = expand(tabs[:, 2], SUBLANES)

    pad = lambda a: jnp.pad(a.reshape(bd, t, nh, e), ((0, 0), (0, SUBLANES - t), (0, 0), (0, 0)))
    last = n_pages - 1
    page_map = lambda b, p, pt: (layer, pt[b, jnp.minimum(p, last)], 0, 0, 0)
    const2 = lambda b, p, pt: (0, 0)
    per_b = lambda b, p, pt: (b, 0, 0, 0)
    out = pl.pallas_call(
        functools.partial(_attn_sample_kernel, n_pages=n_pages, lam_init=lam_init),
        out_shape=jax.ShapeDtypeStruct((bd, ng, rows // 2, e), BF16),
        grid_spec=pltpu.PrefetchScalarGridSpec(
            num_scalar_prefetch=1,
            grid=(bd, n_pages + 1),
            in_specs=[pl.BlockSpec(lam_p.shape, const2),
                      pl.BlockSpec((1, e), const2),
                      pl.BlockSpec((None, ng, rows, e), per_b),
                      pl.BlockSpec((None, None, page, nh, e), page_map),
                      pl.BlockSpec((None, None, page, nh, e), page_map),
                      pl.BlockSpec((None, SUBLANES, nh, e), per_b),
                      pl.BlockSpec((None, SUBLANES, nh, e), per_b),
                      pl.BlockSpec((None, ng, rows, page * HEAD_GROUP),
                                   lambda b, p, pt: (jnp.where(p >= last, 1, 0), 0, 0, 0)),
                      pl.BlockSpec((ng, rows, SUBLANES * HEAD_GROUP), lambda b, p, pt: (0, 0, 0))],
            out_specs=pl.BlockSpec((None, ng, rows // 2, e), per_b),
            scratch_shapes=[pltpu.VMEM((ng, rows, 1), F32), pltpu.VMEM((ng, rows, 1), F32),
                            pltpu.VMEM((ng, rows, e), F32)]),
        compiler_params=_cparams("parallel", "arbitrary"),
        name="diff_attn_sample",
    )(page_table, lam_p, subln.reshape(1, e), qm, cache_k, cache_v, pad(k_new), pad(v_new),
      bias_pages, bias_new)
    return out.reshape(bd, ng, HEAD_GROUP, t, e).transpose(0, 3, 1, 2, 4).reshape(bd * t, nh * e)


def _cross_kernel(q_ref, k_ref, v_ref, o_ref, *, heads, scale):
    q = q_ref[...]
    k = k_ref[...].astype(BF16)
    v = v_ref[...].astype(BF16)
    outs = []
    for h in range(heads):
        sl = slice(h * LANES, (h + 1) * LANES)
        s = lax.dot_general(q[:, sl], k[:, sl], NT_DIMS, preferred_element_type=F32) * scale
        e = jnp.exp(s - jnp.max(s, axis=-1, keepdims=True))
        o = jnp.dot(e.astype(BF16), v[:, sl], preferred_element_type=F32)
        outs.append(o / jnp.sum(e, axis=-1, keepdims=True))
    o_ref[...] = jnp.concatenate(outs, axis=1).astype(o_ref.dtype)


def _cross_attn(q, mk, mv, *, batch, shared_q):
    rows, w = q.shape
    nm = mk.shape[0] // batch
    heads = w // LANES
    if shared_q:
        tq, nq = rows, 1
        q_map = lambda b, i: (0, 0)
    else:
        per = rows // batch
        tq = _pick(per, (512, 256, 128))
        nq = per // tq
        q_map = lambda b, i: (b * nq + i, 0)
    return pl.pallas_call(
        functools.partial(_cross_kernel, heads=heads, scale=LANES ** -0.5),
        out_shape=jax.ShapeDtypeStruct((batch * nq * tq, w), BF16),
        grid=(batch, nq),
        in_specs=[pl.BlockSpec((tq, w), q_map),
                  pl.BlockSpec((nm, w), lambda b, i: (b, 0)),
                  pl.BlockSpec((nm, w), lambda b, i: (b, 0))],
        out_specs=pl.BlockSpec((tq, w), lambda b, i: (b * nq + i, 0)),
        compiler_params=_cparams("parallel", "parallel"),
        name="cross_attn",
    )(q, mk, mv)


FFN_TILE = 256


def _silu_gate(cg, cv):
    return jax.nn.silu(cg) * cv


def _conv_prompt_kernel(ug_ref, uv_ref, bg_ref, bv_ref, wg_ref, wv_ref, cbg_ref, cbv_ref, o_ref,
                        carry_g, carry_v, *, n_valid):
    j, ti = pl.program_id(1), pl.program_id(2)

    @pl.when(j >= n_valid)
    def _():
        o_ref[...] = jnp.zeros(o_ref.shape, o_ref.dtype)

    @pl.when(j < n_valid)
    def _():
        def conv(u_ref, buf_ref, w_ref, cb_ref, carry):
            @pl.when(ti == 0)
            def _():
                carry[SUBLANES - 2:, :] = buf_ref[...]
            u = u_ref[...]
            rows = lax.broadcasted_iota(jnp.int32, u.shape, 0)
            prev = carry[...]
            p1, p2 = prev[SUBLANES - 1:], prev[SUBLANES - 2:SUBLANES - 1]
            s1 = jnp.where(rows == 0, p1, pltpu.roll(u, 1, axis=0))
            s2 = jnp.where(rows == 0, p2, jnp.where(rows == 1, p1, pltpu.roll(u, 2, axis=0)))
            carry[...] = u[u.shape[0] - SUBLANES:]
            w = w_ref[...]
            return cb_ref[...] + w[0:1] * s2 + w[1:2] * s1 + w[2:3] * u

        cg = conv(ug_ref, bg_ref, wg_ref, cbg_ref, carry_g)
        cv = conv(uv_ref, bv_ref, wv_ref, cbv_ref, carry_v)
        o_ref[...] = _silu_gate(cg, cv).astype(o_ref.dtype)


def _conv_gate_prompt(up, buf, conv_w, conv_b, *, batch, f_pad):
    bt, f2 = up.shape
    f = f2 // 2
    t = bt // batch
    tt = _pick(t, (512, 256, 128))
    nt = t // tt
    nv = f // FFN_TILE
    col = lambda j: jnp.minimum(j, nv - 1)
    gate = lambda off: pl.BlockSpec((tt, FFN_TILE), lambda b, j, i: (b * nt + i, col(j) + off))
    bufs = lambda off: pl.BlockSpec((None, CONV_WIDTH - 1, FFN_TILE), lambda b, j, i: (b, 0, col(j) + off))
    taps = lambda off: pl.BlockSpec((CONV_WIDTH, FFN_TILE), lambda b, j, i: (0, col(j) + off))
    cb = lambda off: pl.BlockSpec((1, FFN_TILE), lambda b, j, i: (0, col(j) + off))
    return pl.pallas_call(
        functools.partial(_conv_prompt_kernel, n_valid=nv),
        out_shape=jax.ShapeDtypeStruct((bt, f_pad), BF16),
        grid=(batch, f_pad // FFN_TILE, nt),
        in_specs=[gate(0), gate(nv), bufs(0), bufs(nv), taps(0), taps(nv), cb(0), cb(nv)],
        out_specs=pl.BlockSpec((tt, FFN_TILE), lambda b, j, i: (b * nt + i, j)),
        scratch_shapes=[pltpu.VMEM((SUBLANES, FFN_TILE), F32), pltpu.VMEM((SUBLANES, FFN_TILE), F32)],
        compiler_params=_cparams("parallel", "parallel", "arbitrary"),
        name="conv_gate_prompt",
    )(up, up, buf, buf, conv_w, conv_w, conv_b.reshape(1, f2), conv_b.reshape(1, f2))


def _conv_sample_kernel(x2g, x1g, x0g, x2v, x1v, x0v, wg_ref, wv_ref, cbg_ref, cbv_ref, o_ref, *, n_valid):
    j = pl.program_id(0)

    @pl.when(j >= n_valid)
    def _():
        o_ref[...] = jnp.zeros(o_ref.shape, o_ref.dtype)

    @pl.when(j < n_valid)
    def _():
        wg, wv = wg_ref[...], wv_ref[...]
        cg = cbg_ref[...] + wg[0:1] * x2g[...] + wg[1:2] * x1g[...] + wg[2:3] * x0g[...]
        cv = cbv_ref[...] + wv[0:1] * x2v[...] + wv[1:2] * x1v[...] + wv[2:3] * x0v[...]
        o_ref[...] = _silu_gate(cg, cv).astype(o_ref.dtype)


def _conv_gate_sample(hp, conv_w, conv_b, *, f_pad):
    bd, t2, f2 = hp.shape
    t = t2 - (CONV_WIDTH - 1)
    f = f2 // 2
    nv = f // FFN_TILE
    shifted = [hp[:, s:s + t].reshape(bd * t, f2) for s in range(CONV_WIDTH)]
    col = lambda j: jnp.minimum(j, nv - 1)
    xs = lambda off: pl.BlockSpec((bd * t, FFN_TILE), lambda j: (0, col(j) + off))
    taps = lambda off: pl.BlockSpec((CONV_WIDTH, FFN_TILE), lambda j: (0, col(j) + off))
    cb = lambda off: pl.BlockSpec((1, FFN_TILE), lambda j: (0, col(j) + off))
    return pl.pallas_call(
        functools.partial(_conv_sample_kernel, n_valid=nv),
        out_shape=jax.ShapeDtypeStruct((bd * t, f_pad), BF16),
        grid=(f_pad // FFN_TILE,),
        in_specs=[xs(0)] * 3 + [xs(nv)] * 3 + [taps(0), taps(nv), cb(0), cb(nv)],
        out_specs=pl.BlockSpec((bd * t, FFN_TILE), lambda j: (0, j)),
        compiler_params=_cparams("parallel"),
        name="conv_gate_sample",
    )(*shifted, *shifted, conv_w, conv_w, conv_b.reshape(1, f2), conv_b.reshape(1, f2))


def kernel(x_prompt, x_sample, cache_attn_k, cache_attn_v, cache_mem_k, cache_mem_v, state_ssm_re, state_ssm_im, state_ffn_conv, page_table, mem_prompt, norm_mix, norm_cross, norm_ffn, norm_final, ssm_w_in, ssm_a_re, ssm_a_im, ssm_log_dt, ssm_b_re, ssm_b_im, ssm_c_re, ssm_c_im, ssm_d, ssm_w_glu, attn_w_qkv, attn_lambda, attn_subln, attn_w_o, rel_bias, cross_w_q, cross_w_kv, cross_w_o, ffn_w_up, ffn_conv_w, ffn_conv_b, ffn_w_down):
    bp, seq, dm = x_prompt.shape
    bd, dec_t, _ = x_sample.shape
    depth = norm_mix.shape[0]
    page, nh, hv = cache_attn_k.shape[2:]
    n_pages = page_table.shape[1]
    n_mem, mem_heads, mem_hd = cache_mem_k.shape[2:]
    mw = mem_heads * mem_hd
    d_ff = ffn_w_down.shape[1]
    nb = rel_bias.shape[0]
    assert hv == LANES and mem_hd == LANES and d_ff % FFN_TILE == 0
    attn_scale = (hv // 2) ** -0.5
    f_pad = -(-d_ff // 1024) * 1024

    xp = x_prompt.reshape(bp * seq, dm)
    xs = x_sample.reshape(bd * dec_t, dm)
    mem_b = mem_prompt.reshape(bp * n_mem, dm).astype(BF16)

    blk = min(256, seq)
    far = _far_distance(nb)
    assert blk + 1 >= far and seq % blk == 0
    ii, jj = np.arange(blk)[:, None], np.arange(blk)[None, :]
    diag = np.where(jj > ii, -1, _rel_bucket_np(ii - jj, nb))
    idx_prompt = np.concatenate([diag, _rel_bucket_np(blk + ii - jj, nb)], axis=0)
    past = n_pages * page
    assert page + 1 >= far
    tt =np.arange(SUBLANES)[:, None]
    kk = np.arange(page)[None, :]
    tab_far = np.full((SUBLANES, page), nb - 1)
    tab_last = _rel_bucket_np(past + tt - ((n_pages - 1) * page + kk), nb)
    tab_new = np.where((kk <= tt) & (kk < dec_t) & (tt < dec_t), _rel_bucket_np(tt - kk, nb), -1)
    idx_sample = np.concatenate([tab_far, tab_last, tab_new], axis=0).astype(np.int32)

    outs = {k: [] for k in ("ssm_re_p", "ssm_im_p", "k_p", "v_p", "mk_p", "mv_p", "conv_p",
                            "ssm_re_s", "ssm_im_s", "k_s", "v_s", "conv_s")}
    for i in range(depth):
        j = i // 2
        hp = _rmsnorm(xp, norm_mix[i], BF16)
        hs = _rmsnorm(xs, norm_mix[i], BF16)
        if i % 2 == 0:
            w_in = ssm_w_in[j].astype(BF16)
            w_glu = ssm_w_glu[j].astype(BF16)
            prep = _ssm_prep(ssm_a_re[j], ssm_a_im[j], ssm_log_dt[j], ssm_b_re[j], ssm_b_im[j],
                             ssm_c_re[j], ssm_c_im[j])
            up_ = _mm(hp, w_in, name="ssm_in_p")
            us_ = _mm(hs, w_in, name="ssm_in_s")
            zp, hrp, hip = _ssm_mixer_prompt(up_, prep, ssm_d[j], bp)
            zs, hrs, his = _ssm_mixer_sample(us_, prep, ssm_d[j], state_ssm_re[j], state_ssm_im[j])
            xp = _glu_mm(zp, w_glu, xp)
            xs = _glu_mm(zs, w_glu, xs)
            outs["ssm_re_p"].append(hrp)
            outs["ssm_im_p"].append(hip)
            outs["ssm_re_s"].append(hrs)
            outs["ssm_im_s"].append(his)
        else:
            lam_init = 0.8 - 0.6 * math.exp(-0.3 * i)
            w_qkv = attn_w_qkv[j].astype(BF16)
            w_o = attn_w_o[j].astype(BF16)
            width = nh * hv
            qkv = lambda h, tag: (_mm(h, w_qkv, n=width, out_dtype=BF16, scale=attn_scale, name="q_" + tag),
                                  _mm(h, w_qkv, n=width, col_off=width, name="k_" + tag),
                                  _mm(h, w_qkv, n=width, col_off=2 * width, name="v_" + tag))
            qp, kp_, vp_ = qkv(hp, "p")
            qs, ks_, vs_ = qkv(hs, "s")
            bias_p = _bias_tables(rel_bias, idx_prompt).reshape(nh, 2, blk, blk)
            tabs_s = _bias_tables(rel_bias, idx_sample).reshape(nh, 3, SUBLANES, page)
            ap = _attn_prompt(qp, kp_, vp_, bias_p, attn_lambda[j], attn_subln[j],
                              batch=bp, lam_init=lam_init, blk=blk)
            as_ = _attn_sample(qs, ks_, vs_, cache_attn_k, cache_attn_v, j, page_table, tabs_s,
                               attn_lambda[j], attn_subln[j], lam_init=lam_init)
            xp = _mm(ap, w_o, res=xp, name="attn_o_p")
            xs = _mm(as_, w_o, res=xs, name="attn_o_s")
            outs["k_p"].append(kp_.reshape(bp, seq, nh, hv))
            outs["v_p"].append(vp_.reshape(bp, seq, nh, hv))
            outs["k_s"].append(ks_.reshape(bd, dec_t, nh, hv))
            outs["v_s"].append(vs_.reshape(bd, dec_t, nh, hv))

        hp = _rmsnorm(xp, norm_cross[i], BF16)
        hs = _rmsnorm(xs, norm_cross[i], BF16)
        w_q = cross_w_q[i].astype(BF16)
        w_kv = cross_w_kv[i].astype(BF16)
        w_co = cross_w_o[i].astype(BF16)
        mkp = _mm(mem_b, w_kv, n=mw, name="mem_k")
        mvp = _mm(mem_b, w_kv, n=mw, col_off=mw, name="mem_v")
        outs["mk_p"].append(mkp.reshape(bp, n_mem, mem_heads, mem_hd))
        outs["mv_p"].append(mvp.reshape(bp, n_mem, mem_heads, mem_hd))
        cq_p = _mm(hp, w_q, out_dtype=BF16, name="cross_q_p")
        cq_s = _mm(hs, w_q, out_dtype=BF16, name="cross_q_s")
        co_p = _cross_attn(cq_p, mkp, mvp, batch=bp, shared_q=False)
        co_all = _cross_attn(cq_s, cache_mem_k[i].reshape(bd * n_mem, mw), cache_mem_v[i].reshape(bd * n_mem, mw),
                             batch=bd, shared_q=True)
        co_all = co_all.reshape(bd, bd, dec_t, mw)
        co_s = jnp.stack([co_all[b, b] for b in range(bd)]).reshape(bd * dec_t, mw)
        xp = _mm(co_p, w_co, res=xp, name="cross_o_p")
        xs = _mm(co_s, w_co, res=xs, name="cross_o_s")

        hp = _rmsnorm(xp, norm_ffn[i], BF16)
        hs = _rmsnorm(xs, norm_ffn[i], BF16)
        w_up = ffn_w_up[i].astype(BF16)
        w_down = jnp.pad(ffn_w_down[i].astype(BF16), ((0, f_pad - d_ff), (0, 0)))
        up_p = _mm(hp, w_up, name="ffn_up_p")
        up_s = _mm(hs, w_up, name="ffn_up_s")
        buf0 = jnp.zeros((bp, CONV_WIDTH - 1, 2 * d_ff), F32)
        gp = _conv_gate_prompt(up_p, buf0, ffn_conv_w[i], ffn_conv_b[i], batch=bp, f_pad=f_pad)
        hp_s = jnp.concatenate([state_ffn_conv[i], up_s.reshape(bd, dec_t, 2 * d_ff)], axis=1)
        gs = _conv_gate_sample(hp_s, ffn_conv_w[i], ffn_conv_b[i], f_pad=f_pad)
        xp = _mm(gp, w_down, res=xp, name="ffn_down_p")
        xs = _mm(gs, w_down, res=xs, name="ffn_down_s")
        hp_p = jnp.concatenate([buf0, up_p.reshape(bp, seq, 2 * d_ff)[:, -(CONV_WIDTH - 1):]], axis=1)
        outs["conv_p"].append(hp_p[:, -(CONV_WIDTH - 1):])
        outs["conv_s"].append(hp_s[:, -(CONV_WIDTH - 1):])

    y_prompt = _rmsnorm(xp, norm_final, F32).reshape(bp, seq, dm)
    y_sample = _rmsnorm(xs, norm_final, F32).reshape(bd, dec_t, dm)
    st = lambda k: jnp.stack(outs[k])
    return (y_prompt, y_sample, st("ssm_re_p"), st("ssm_im_p"), st("k_p"), st("v_p"),
            st("mk_p"), st("mv_p"), st("conv_p"), st("ssm_re_s"), st("ssm_im_s"),
            st("k_s"), st("v_s"), st("conv_s"))
```

```python
import functools
import math

import jax
import jax.numpy as jnp
import numpy as np
from jax import lax
from jax.experimental import pallas as pl
from jax.experimental.pallas import tpu as pltpu

F32 = jnp.float32
BF16 = jnp.bfloat16

SSM_GROUP = 16
MAX_DISTANCE = 128
RMS_EPS = 1e-6
SUBLN_EPS = 1e-5
CONV_WIDTH = 3

LANES = 128
SUBLANES = 8
VMEM_LIMIT_BYTES = 56 * 1024 * 1024

MASK_VALUE = -1e30
NT_DIMS = (((1,), (1,)), ((), ()))


def _cparams(*sem):
    return pltpu.CompilerParams(dimension_semantics=sem, vmem_limit_bytes=VMEM_LIMIT_BYTES)


def _pick(n, cands):
    for c in cands:
        if n % c == 0:
            return c
    return n


def _rmsnorm_kernel(x_ref, g_ref, o_ref):
    x = x_ref[...]
    ms = jnp.mean(x * x, axis=-1, keepdims=True)
    o_ref[...] = (x * lax.rsqrt(ms + RMS_EPS) * g_ref[...]).astype(o_ref.dtype)


def _rmsnorm(x, g, out_dtype):
    m, d = x.shape
    tm = _pick(m, (256,))
    return pl.pallas_call(
        _rmsnorm_kernel,
        out_shape=jax.ShapeDtypeStruct((m, d), out_dtype),
        grid=(m // tm,),
        in_specs=[pl.BlockSpec((tm, d), lambda i: (i, 0)),
                  pl.BlockSpec((1, d), lambda i: (0, 0))],
        out_specs=pl.BlockSpec((tm, d), lambda i: (i, 0)),
        compiler_params=_cparams("parallel"),
        name="rmsnorm",
    )(x, g.reshape(1, d))


def _mm_kernel(*refs, nk, scale, has_res):
    if has_res:
        x_ref, w_ref, r_ref, o_ref = refs[:4]
        rest = refs[4:]
    else:
        x_ref, w_ref, o_ref = refs[:3]
        r_ref = None
        rest = refs[3:]

    def finish(acc):
        if scale is not None:
            acc = acc * scale
        if r_ref is not None:
            acc = r_ref[...] + acc
        o_ref[...] = acc.astype(o_ref.dtype)

    part = jnp.dot(x_ref[...], w_ref[...], preferred_element_type=F32)
    if nk == 1:
        finish(part)
        return
    acc_ref, = rest
    k = pl.program_id(2)

    @pl.when(k == 0)
    def _():
        acc_ref[...] = part

    @pl.when(k > 0)
    def _():
        acc_ref[...] += part

    @pl.when(k == nk - 1)
    def _():
        finish(acc_ref[...])


MAX_K_BLOCK = 6144


def _mm_tiles(m, n, k):
    tm = _pick(m, (1024,))
    if m >= 1024:
        tn = _pick(n, (512, 256, 128))
    else:
        tn = _pick(n, (1024, 512, 256, 128))
    tk = k if k <= MAX_K_BLOCK else _pick(k, range(MAX_K_BLOCK - MAX_K_BLOCK % LANES, 0, -LANES))
    return tm, tn, tk


def _mm(x, w, layer, *, n=None, col_off=0, out_dtype=F32, res=None, scale=None, name="mm"):
    m, k = x.shape
    n = w.shape[2] if n is None else n
    tm, tn, tk = _mm_tiles(m, n, k)
    nk = k // tk
    assert col_off % tn == 0 and m % tm == 0 and n % tn == 0 and k % tk == 0
    cb = col_off // tn
    in_specs = [pl.BlockSpec((tm, tk), lambda i, j, kk: (i, kk)),
                pl.BlockSpec((None, tk, tn), lambda i, j, kk: (layer, kk, j + cb))]
    args = [x, w]
    if res is not None:
        in_specs.append(pl.BlockSpec((tm, tn), lambda i, j, kk: (i, j)))
        args.append(res)
    return pl.pallas_call(
        functools.partial(_mm_kernel, nk=nk, scale=scale, has_res=res is not None),
        out_shape=jax.ShapeDtypeStruct((m, n), out_dtype),
        grid=(m // tm, n // tn, nk),
        in_specs=in_specs,
        out_specs=pl.BlockSpec((tm, tn), lambda i, j, kk: (i, j)),
        scratch_shapes=[pltpu.VMEM((tm, tn), F32)] if nk > 1 else [],
        compiler_params=_cparams("parallel", "parallel", "arbitrary"),
        name=name,
    )(*args)


def _glu_kernel(x_ref, wa_ref, wb_ref, r_ref, o_ref):
    x = x_ref[...]
    a = jnp.dot(x, wa_ref[...], preferred_element_type=F32)
    b = jnp.dot(x, wb_ref[...], preferred_element_type=F32)
    o_ref[...] = r_ref[...] + a * jax.nn.sigmoid(b)


def _glu_mm(x, w, layer, res):
    m, k = x.shape
    d = w.shape[2] // 2
    tm = _pick(m, (1024,))
    tn = _pick(d, (512, 256, 128))
    nj = d // tn
    return pl.pallas_call(
        _glu_kernel,
        out_shape=jax.ShapeDtypeStruct((m, d), F32),
        grid=(m // tm, nj),
        in_specs=[pl.BlockSpec((tm, k), lambda i, j: (i, 0)),
                  pl.BlockSpec((None, k, tn), lambda i, j: (layer, 0, j)),
                  pl.BlockSpec((None, k, tn), lambda i, j: (layer, 0, j + nj)),
                  pl.BlockSpec((tm, tn), lambda i, j: (i, j))],
        out_specs=pl.BlockSpec((tm, tn), lambda i, j: (i, j)),
        compiler_params=_cparams("parallel", "parallel"),
        name="glu_mm",
    )(x, w, w, res)


def _ssm_discretise(ar, ai, log_dt):
    dt = jnp.exp(log_dt)
    mag = jnp.exp(ar * dt)
    return mag * jnp.cos(ai * dt), mag * jnp.sin(ai * dt)


def _ssm_prep_kernel(are_ref, aim_ref, ldt_ref, arc_ref, aic_ref, ldc_ref, btr_ref, bti_ref, ctr_ref, cti_ref,
                     lr_ref, li_ref, wb_ref, wc_ref):
    lr_ref[...], li_ref[...] = _ssm_discretise(are_ref[...], aim_ref[...], ldt_ref[...])
    ar, ai = arc_ref[...], aic_ref[...]
    lr, li = _ssm_discretise(ar, ai, ldc_ref[...])
    den = ar * ar + ai * ai
    nr = lr - 1.0
    kr = (nr * ar + li * ai) / den
    ki = (li * ar - nr * ai) / den
    btr, bti = btr_ref[...], bti_ref[...]
    rows = SUBLANES * SSM_GROUP
    b_re = (kr * btr - ki * bti).reshape(rows, LANES)
    b_im = (kr * bti + ki * btr).reshape(rows, LANES)
    c_re = ctr_ref[...].reshape(rows, LANES)
    c_im = -cti_ref[...].reshape(rows, LANES)
    row_group = lax.broadcasted_iota(jnp.int32, (rows, LANES), 0) // SSM_GROUP
    lane_half = lax.broadcasted_iota(jnp.int32, (rows, LANES), 1) // (LANES // 2)
    nblk = SUBLANES // 2
    for half, (bsrc, csrc) in enumerate(((b_re, c_re), (b_im, c_im))):
        for mblk in range(nblk):
            keep = row_group == 2 * mblk + lane_half
            sl = slice((half * nblk + mblk) * LANES, (half * nblk + mblk + 1) * LANES)
            wb_ref[:, sl] = jnp.where(keep, bsrc, 0.0).astype(wb_ref.dtype)
            wc_ref[:, sl] = jnp.where(keep, csrc, 0.0).astype(wc_ref.dtype)


def _ssm_prep(a_re, a_im, log_dt, b_re, b_im, c_re, c_im):
    g, p = a_re.shape
    assert 2 * p == LANES and g % SUBLANES == 0
    nt = g // SUBLANES
    width = SUBLANES * LANES
    dup = lambda a: jnp.concatenate([a, a], axis=-1)
    btr = dup(jnp.swapaxes(b_re, 1, 2))
    bti = dup(jnp.swapaxes(b_im, 1, 2))
    ldt = jnp.broadcast_to(log_dt[:, None], (g, LANES))
    on_rows = lambda a: jnp.broadcast_to(a[:, None, :], (g, SSM_GROUP, LANES))
    vec =pl.BlockSpec((SUBLANES, LANES), lambda j: (j, 0))
    cube = pl.BlockSpec((SUBLANES, SSM_GROUP, LANES), lambda j: (j, 0, 0))
    wide = pl.BlockSpec((None, SUBLANES * SSM_GROUP, width), lambda j: (j, 0, 0))
    return pl.pallas_call(
        _ssm_prep_kernel,
        out_shape=[jax.ShapeDtypeStruct((g, LANES), F32), jax.ShapeDtypeStruct((g, LANES), F32),
                   jax.ShapeDtypeStruct((nt, SUBLANES * SSM_GROUP, width), BF16),
                   jax.ShapeDtypeStruct((nt, SUBLANES * SSM_GROUP, width), BF16)],
        grid=(nt,),
        in_specs=[vec, vec, vec, cube, cube, cube, cube, cube, cube, cube],
        out_specs=[vec, vec, wide, wide],
        compiler_params=_cparams("parallel"),
        name="ssm_prep",
    )(dup(a_re), dup(a_im), ldt, on_rows(dup(a_re)), on_rows(dup(a_im)), on_rows(ldt),
      btr, bti, dup(c_re), dup(c_im))


SCAN_CHUNK = 128
SCAN_UNROLL = 4


def _ssm_scan_kernel(*refs, paired, steps):
    if paired:
        (u_ref, wb0_ref, wb1_ref, wc0_ref, wc1_ref, lr_ref, li_ref, d_ref, h0_ref,
         z_ref, hf_ref, bu_ref, hs_ref) = refs
    else:
        (u_ref, wb0_ref, wc0_ref, lr_ref, li_ref, d_ref, h0_ref,
         z_ref, hf_ref, bu_ref, hs_ref) = refs
    half = hs_ref.shape[1] // 2

    @pl.when(pl.program_id(1) == 0)
    def _():
        hs_ref[...] = h0_ref[...]

    u = u_ref[...]
    if paired:
        odd = lax.broadcasted_iota(jnp.int32, (u.shape[0], 1), 0) % 2 == 1
        ub = jnp.concatenate([jnp.where(odd, 0.0, u), jnp.where(odd, u, 0.0)], axis=1).astype(BF16)
        wb = jnp.concatenate([wb0_ref[...], wb1_ref[...]], axis=0)
    else:
        ub, wb = u.astype(BF16), wb0_ref[...]
    bu_ref[...] = jnp.dot(ub, wb, preferred_element_type=F32)

    lr, li = lr_ref[...], li_ref[...]

    def step(t, carry):
        hr, hi = carry
        row = pl.multiple_of(t * SUBLANES, SUBLANES)
        blk = bu_ref[pl.ds(row, SUBLANES), :]
        nr = hr * lr - hi * li + blk[:, :half]
        ni = hr * li + hi * lr + blk[:, half:]
        bu_ref[pl.ds(row, SUBLANES), :] = jnp.concatenate([nr, ni], axis=1)
        return nr, ni

    h0 = hs_ref[...]
    hr, hi = lax.fori_loop(0, steps, step, (h0[:, :half], h0[:, half:]), unroll=SCAN_UNROLL)
    hfin = jnp.concatenate([hr, hi], axis=1)
    hs_ref[...] = hfin
    hf_ref[...] = hfin

    hs = bu_ref[...].astype(BF16)
    if paired:
        wc = jnp.concatenate([wc0_ref[...], wc1_ref[...]], axis=0)
        y2 = lax.dot_general(hs, wc, NT_DIMS, preferred_element_type=F32)
        y = jnp.where(odd, y2[:, LANES:], y2[:, :LANES])
    else:
        y = lax.dot_general(hs, wc0_ref[...], NT_DIMS, preferred_element_type=F32)
    y = (y.reshape(steps, SUBLANES, LANES) + d_ref[...][None] * u.reshape(steps, SUBLANES, LANES))
    z_ref[...] = jax.nn.gelu(y).reshape(steps * SUBLANES, LANES).astype(z_ref.dtype)


def _ssm_scan(u_rows, wb, wc, lr_slab, li_slab, d_slab, h0, *, paired):
    rows, c = u_rows.shape
    t = rows // SUBLANES
    nj = c // LANES
    lt = _pick(t, (SCAN_CHUNK,))
    width = wb.shape[2]
    off = wb.shape[0] // 2
    u_spec = pl.BlockSpec((lt * SUBLANES, LANES), lambda j, s: (s, j))
    w_lo = pl.BlockSpec((None, LANES, width), lambda j, s: (j, 0, 0))
    w_hi = pl.BlockSpec((None, LANES, width), lambda j, s: (j + off, 0, 0))
    slab = lambda n: pl.BlockSpec((None, SUBLANES, n), lambda j, s: (j, 0, 0))
    if paired:
        in_specs = [u_spec, w_lo, w_hi, w_lo, w_hi]
        args = [u_rows, wb, wb, wc, wc]
    else:
        in_specs = [u_spec, w_lo, w_lo]
        args = [u_rows, wb, wc]
    in_specs += [slab(width // 2), slab(width // 2), slab(LANES), slab(width)]
    args += [lr_slab, li_slab, d_slab, h0]
    return pl.pallas_call(
        functools.partial(_ssm_scan_kernel, paired=paired, steps=lt),
        out_shape=[jax.ShapeDtypeStruct((rows, c), BF16),
                   jax.ShapeDtypeStruct((nj, SUBLANES, width), F32)],
        grid=(nj, t // lt),
        in_specs=in_specs,
        out_specs=[u_spec, slab(width)],
        scratch_shapes=[pltpu.VMEM((lt * SUBLANES, width), F32), pltpu.VMEM((SUBLANES, width), F32)],
        compiler_params=_cparams("parallel", "arbitrary"),
        name="ssm_scan",
    )(*args)


def _ssm_tiles(v, nt):
    return v[:, : LANES // 2].reshape(nt, SUBLANES * (LANES // 2))


def _ssm_mixer_prompt(u, prep, d, batch):
    lr, li, wb, wc = prep
    nt = wb.shape[0]
    bt, dm = u.shape
    t = bt // batch
    assert 2 * batch == SUBLANES and nt % 2 == 0
    hw = SUBLANES * (LANES // 2)
    u_rows = u.reshape(batch, t, 2, dm // 2).transpose(1, 0, 2, 3).reshape(t * SUBLANES, dm // 2)

    def pair_slab(tiles):
        n = tiles.shape[1]
        s = jnp.stack([tiles[: nt // 2], tiles[nt // 2:]], axis=1)
        return jnp.broadcast_to(s[:, None], (nt // 2, batch, 2, n)).reshape(nt // 2, SUBLANES, n)

    z_rows, hfin = _ssm_scan(u_rows, wb, wc, pair_slab(_ssm_tiles(lr, nt)), pair_slab(_ssm_tiles(li, nt)),
                             pair_slab(d.reshape(nt, LANES)), jnp.zeros((nt // 2, SUBLANES, 2 * hw), F32),
                             paired=True)
    z = z_rows.reshape(t, batch, 2, dm // 2).transpose(1, 0, 2, 3).reshape(bt, dm)

    def states(h):
        h = h.reshape(nt // 2, batch, 2, SUBLANES, LANES // 2).transpose(1, 2, 0, 3, 4)
        return h.reshape(batch, nt * SUBLANES, LANES // 2)

    return z, states(hfin[..., :hw]), states(hfin[..., hw:])


def _ssm_mixer_sample(u, prep, d, h0_re, h0_im):
    lr, li, wb, wc = prep
    nt = wb.shape[0]
    bd = h0_re.shape[0]
    bt, dm = u.shape
    t = bt // bd
    assert bd == SUBLANES
    hw = SUBLANES * (LANES // 2)
    u_rows = u.reshape(bd, t, dm).transpose(1, 0, 2).reshape(t * bd, dm)
    slab = lambda tiles: jnp.broadcast_to(tiles[:, None], (nt, SUBLANES, tiles.shape[1]))
    to_tiles = lambda h: h.reshape(bd, nt, hw).transpose(1, 0, 2)
    h0 = jnp.concatenate([to_tiles(h0_re), to_tiles(h0_im)], axis=-1)
    z_rows, hfin = _ssm_scan(u_rows, wb, wc, slab(_ssm_tiles(lr, nt)), slab(_ssm_tiles(li, nt)),
                             slab(d.reshape(nt, LANES)), h0, paired=False)
    z = z_rows.reshape(t, bd, dm).transpose(1, 0, 2).reshape(bt, dm)
    states = lambda h: h.transpose(1, 0, 2).reshape(bd, nt * SUBLANES, LANES // 2)
    return z, states(hfin[..., :hw]), states(hfin[..., hw:])


def _rel_bucket_np(dist, num_buckets):
    n = np.maximum(dist, 0)
    max_exact = num_buckets // 2
    nf = np.maximum(n, 1).astype(np.float32)
    large = max_exact + (np.log(nf / np.float32(max_exact)) / np.float32(math.log(MAX_DISTANCE / max_exact))
                         * np.float32(num_buckets - max_exact)).astype(np.int32)
    large = np.minimum(large, num_buckets - 1)
    return np.where(n < max_exact, n, large).astype(np.int32)


def _far_distance(num_buckets):
    b = _rel_bucket_np(np.arange(4 * MAX_DISTANCE), num_buckets)
    return int(np.max(np.nonzero(b != num_buckets - 1)[0])) + 1


def _bias_kernel(rb_ref, idx_ref, o_ref, *, num_buckets):
    h = pl.program_id(0)
    idx = idx_ref[...]
    acc = jnp.full(idx.shape, MASK_VALUE, F32)
    for b in range(num_buckets):
        acc = jnp.where(idx == b, rb_ref[b, h], acc)
    o_ref[...] = acc


def _bias_tables(rel_bias, idx):
    nb, nh = rel_bias.shape
    r, c = idx.shape
    return pl.pallas_call(
        functools.partial(_bias_kernel, num_buckets=nb),
        out_shape=jax.ShapeDtypeStruct((nh, r, c), F32),
        grid=(nh,),
        in_specs=[pl.BlockSpec(memory_space=pltpu.SMEM),
                  pl.BlockSpec((r, c), lambda h: (0, 0))],
        out_specs=pl.BlockSpec((None, r, c), lambda h: (h, 0, 0)),
        compiler_params=_cparams("parallel"),
        name="rel_bias_tables",
    )(rel_bias, jnp.asarray(idx))


def _diff_lambda(lam_ref, lam_init):
    lp = lam_ref[...]
    e1 = jnp.exp(jnp.sum(lp[0:1] * lp[1:2], keepdims=True))
    e2 = jnp.exp(jnp.sum(lp[2:3] * lp[3:4], keepdims=True))
    return e1 - e2 + lam_init


def _diff_merge(o1, o2, lam, subln, lam_init):
    d = o1 - lam * o2
    d = d * lax.rsqrt(jnp.mean(d * d, axis=-1, keepdims=True) + SUBLN_EPS) * subln
    return d * (1.0 - lam_init)


ATTN_HEADS_PER_STEP = 2


def _attn_prompt_kernel(lam_ref, subln_ref, q_ref, k_ref, v_ref, bias_ref, o_ref,
                        kb_ref, vt_ref, m_ref, l_ref, acc_ref, *, blk, lam_init):
    qi = pl.program_id(2)
    heads = kb_ref.shape[0]
    e = LANES

    @pl.when(qi == 0)
    def _():
        for h in range(heads):
            for j in range(kb_ref.shape[1]):
                rows, cols = slice(j * blk, (j + 1) * blk), slice(h * e, (h + 1) * e)
                kb_ref[h, j] = k_ref[rows, cols].astype(BF16)
                vt_ref[h, j] = v_ref[rows, cols].T.astype(BF16)

    feat = lax.broadcasted_iota(jnp.int32, (e, blk), 0)
    qts = []
    for h in range(heads):
        qt = q_ref[:, h * e:(h + 1) * e].astype(F32).T
        qts.append(jnp.concatenate([jnp.where(feat < e // 2, qt, 0.0), jnp.where(feat >= e // 2, qt, 0.0)],
                                   axis=1).astype(BF16))

    m_ref[...] = jnp.full(m_ref.shape, MASK_VALUE, F32)
    l_ref[...] = jnp.zeros(l_ref.shape, F32)
    acc_ref[...] = jnp.zeros(acc_ref.shape, F32)

    def attend(kb, tile):
        sts, ps, corrs = [], [], []
        for h in range(heads):
            sts.append(jnp.dot(kb_ref[h, kb], qts[h], preferred_element_type=F32))
        for h in range(heads):
            if tile is None:
                bias = bias_ref[h, 1, 0:1, blk - 1:blk]
            else:
                bt = bias_ref[h, tile]
                bias = jnp.concatenate([bt, bt], axis=1)
            st = sts[h] + bias
            m_prev = m_ref[h]
            m_new = jnp.maximum(m_prev, jnp.max(st, axis=0, keepdims=True))
            corr = jnp.exp(m_prev - m_new)
            p = jnp.exp(st - m_new)
            l_ref[h] = l_ref[h] * corr + jnp.sum(p, axis=0, keepdims=True)
            m_ref[h] = m_new
            ps.append(p.astype(BF16))
            corrs.append(corr)
        for h in range(heads):
            acc_ref[h] = acc_ref[h] * corrs[h] + jnp.dot(vt_ref[h, kb], ps[h], preferred_element_type=F32)

    def far_body(kb, carry):
        attend(kb, None)
        return carry

    lax.fori_loop(0, jnp.maximum(qi - 1, 0), far_body, 0)

    @pl.when(qi >= 1)
    def _():
        attend(qi - 1, 1)

    attend(qi, 0)

    lam = _diff_lambda(lam_ref, lam_init)
    for h in range(heads):
        ot = acc_ref[h] / l_ref[h]
        dt = ot[:, :blk] - lam * ot[:, blk:]
        dt = dt * lax.rsqrt(jnp.mean(dt * dt, axis=0, keepdims=True) + SUBLN_EPS)
        o_ref[:, h * e:(h + 1) * e] = (dt.T * subln_ref[...] * (1.0 - lam_init)).astype(o_ref.dtype)


def _attn_prompt(q, k, v, bias, lam_p, subln, *, batch, lam_init, blk):
    bs, he = q.shape
    s = bs // batch
    e = LANES
    nh = he // e
    nq = s // blk
    hps = ATTN_HEADS_PER_STEP if nh % ATTN_HEADS_PER_STEP == 0 else 1
    w = hps * e
    return pl.pallas_call(
        functools.partial(_attn_prompt_kernel, blk=blk, lam_init=lam_init),
        out_shape=jax.ShapeDtypeStruct((bs, he), BF16),
        grid=(batch, nh // hps, nq),
        in_specs=[pl.BlockSpec(lam_p.shape, lambda b, h, i: (0, 0)),
                  pl.BlockSpec((1, e), lambda b, h, i: (0, 0)),
                  pl.BlockSpec((blk, w), lambda b, h, i: (b * nq + i, h)),
                  pl.BlockSpec((s, w), lambda b, h, i: (b, h)),
                  pl.BlockSpec((s, w), lambda b, h, i: (b, h)),
                  pl.BlockSpec((hps, 2, blk, blk), lambda b, h, i: (h, 0, 0, 0))],
        out_specs=pl.BlockSpec((blk, w), lambda b, h, i: (b * nq + i, h)),
        scratch_shapes=[pltpu.VMEM((hps, nq, blk, e), BF16), pltpu.VMEM((hps, nq, e, blk), BF16),
                        pltpu.VMEM((hps, 1, 2 * blk), F32), pltpu.VMEM((hps, 1, 2 * blk), F32),
                        pltpu.VMEM((hps, e, 2 * blk), F32)],
        compiler_params=_cparams("parallel", "parallel", "arbitrary"),
        name="diff_attn_prompt",
    )(lam_p, subln.reshape(1, e), q, k, v, bias)


HEAD_GROUP = SUBLANES


def _attn_sample_kernel(pt_ref, lam_ref, subln_ref, q_ref, ck_ref, cv_ref, kn_ref, vn_ref,
                        bias_ref, biasn_ref, o_ref, m_ref, l_ref, acc_ref, *, n_pages, lam_init):
    p = pl.program_id(1)
    ng = q_ref.shape[0]

    @pl.when(p == 0)
    def _():
        m_ref[...] = jnp.full(m_ref.shape, MASK_VALUE, F32)
        l_ref[...] = jnp.zeros(l_ref.shape, F32)
        acc_ref[...] = jnp.zeros(acc_ref.shape, F32)

    def attend(k_ref, v_ref, b_ref):
        keys = k_ref.shape[0]
        group = lambda ref, g: ref[:, g * HEAD_GROUP:(g + 1) * HEAD_GROUP, :].reshape(
            keys * HEAD_GROUP, LANES).astype(BF16)
        scores = [lax.dot_general(q_ref[g], group(k_ref, g), NT_DIMS, preferred_element_type=F32)
                  for g in range(ng)]
        probs, corrs = [], []
        for g in range(ng):
            s = scores[g] + b_ref[g]
            m_prev = m_ref[g]
            m_new = jnp.maximum(m_prev, jnp.max(s, axis=-1, keepdims=True))
            corr = jnp.exp(m_prev - m_new)
            pe = jnp.exp(s - m_new)
            l_ref[g] = l_ref[g] * corr + jnp.sum(pe, axis=-1, keepdims=True)
            m_ref[g] = m_new
            probs.append(pe.astype(BF16))
            corrs.append(corr)
        for g in range(ng):
            acc_ref[g] = acc_ref[g] * corrs[g] + jnp.dot(probs[g], group(v_ref, g), preferred_element_type=F32)

    @pl.when(p < n_pages)
    def _():
        attend(ck_ref, cv_ref, bias_ref)

    @pl.when(p == n_pages)
    def _():
        attend(kn_ref, vn_ref, biasn_ref)
        o = acc_ref[...] / l_ref[...]
        half = o.shape[1] // 2
        lam = _diff_lambda(lam_ref, lam_init)
        o_ref[...] = _diff_merge(o[:, :half], o[:, half:], lam, subln_ref[...], lam_init).astype(o_ref.dtype)


def _attn_sample(q, k_new, v_new, cache_k, cache_v, layer, page_table, tabs, lam_p, subln, *, lam_init):
    bd, n_pages = page_table.shape
    page, nh, e = cache_k.shape[2:]
    t = q.shape[0] // bd
    assert e == LANES and nh % HEAD_GROUP == 0 and t <= SUBLANES
    ng = nh // HEAD_GROUP
    hh = e // 2
    rows = 2 * HEAD_GROUP * t

    q5 = q.reshape(bd, t, ng, HEAD_GROUP, e).transpose(0, 2, 3, 1, 4)
    lane = jnp.arange(e) < hh
    qm = jnp.stack([jnp.where(lane, q5, 0), jnp.where(lane, 0, q5)], axis=2)
    qm = qm.reshape(bd, ng, rows, e)

    same = np.arange(HEAD_GROUP)[:, None] == np.arange(HEAD_GROUP)[None, :]

    def expand(tab, keys):
        tb = tab[:, :t, :keys].reshape(ng, HEAD_GROUP, t, keys)
        full = jnp.where(same[None, :, None, None, :], tb[..., None], MASK_VALUE)
        full = jnp.broadcast_to(full[:, None], (ng, 2, HEAD_GROUP, t, keys, HEAD_GROUP))
        return full.reshape(ng, rows, keys * HEAD_GROUP)

    bias_pages = jnp.stack([expand(tabs[:, 0], page), expand(tabs[:, 1], page)])
    bias_new = expand(tabs[:, 2], SUBLANES)

    pad = lambda a: jnp.pad(a.reshape(bd, t, nh, e), ((0, 0), (0, SUBLANES - t), (0, 0), (0, 0)))
    last = n_pages - 1
    page_map = lambda b, p, pt: (layer, pt[b, jnp.minimum(p, last)], 0, 0, 0)
    const2 = lambda b, p, pt: (0, 0)
    per_b = lambda b, p, pt: (b, 0, 0, 0)
    out = pl.pallas_call(
        functools.partial(_attn_sample_kernel, n_pages=n_pages, lam_init=lam_init),
        out_shape=jax.ShapeDtypeStruct((bd, ng, rows // 2, e), BF16),
        grid_spec=pltpu.PrefetchScalarGridSpec(
            num_scalar_prefetch=1,
            grid=(bd, n_pages + 1),
            in_specs=[pl.BlockSpec(lam_p.shape, const2),
                      pl.BlockSpec((1, e), const2),
                      pl.BlockSpec((None, ng, rows, e), per_b),
                      pl.BlockSpec((None, None, page, nh, e), page_map),
                      pl.BlockSpec((None, None, page, nh, e), page_map),
                      pl.BlockSpec((None, SUBLANES, nh, e), per_b),
                      pl.BlockSpec((None, SUBLANES, nh, e), per_b),
                      pl.BlockSpec((None, ng, rows, page * HEAD_GROUP),
                                   lambda b, p, pt: (jnp.where(p >= last, 1, 0), 0, 0, 0)),
                      pl.BlockSpec((ng, rows, SUBLANES * HEAD_GROUP), lambda b, p, pt: (0, 0, 0))],
            out_specs=pl.BlockSpec((None, ng, rows // 2, e), per_b),
            scratch_shapes=[pltpu.VMEM((ng, rows, 1), F32), pltpu.VMEM((ng, rows, 1), F32),
                            pltpu.VMEM((ng, rows, e), F32)]),
        compiler_params=_cparams("parallel", "arbitrary"),
        name="diff_attn_sample",
    )(page_table, lam_p, subln.reshape(1, e), qm, cache_k, cache_v, pad(k_new), pad(v_new),
      bias_pages, bias_new)
    return out.reshape(bd, ng, HEAD_GROUP, t, e).transpose(0, 3, 1, 2, 4).reshape(bd * t, nh * e)


def _cross_kernel(q_ref, k_ref, v_ref, o_ref, *, heads, scale):
    q = q_ref[...]
    k = k_ref[...].astype(BF16)
    v = v_ref[...].astype(BF16)
    outs = []
    for h in range(heads):
        sl = slice(h * LANES, (h + 1) * LANES)
        s = lax.dot_general(q[:, sl], k[:, sl], NT_DIMS, preferred_element_type=F32) * scale
        e = jnp.exp(s - jnp.max(s, axis=-1, keepdims=True))
        o = jnp.dot(e.astype(BF16), v[:, sl], preferred_element_type=F32)
        outs.append(o / jnp.sum(e, axis=-1, keepdims=True))
    o_ref[...] = jnp.concatenate(outs, axis=1).astype(o_ref.dtype)


def _cross_attn(q, mk, mv, *, batch, shared_q):
    rows, w = q.shape
    nm = mk.shape[0] // batch
    heads = w // LANES
    if shared_q:
        tq, nq = rows, 1
        q_map = lambda b, i: (0, 0)
    else:
        per = rows // batch
        tq = _pick(per, (512, 256, 128))
        nq = per // tq
        q_map = lambda b, i: (b * nq + i, 0)
    return pl.pallas_call(
        functools.partial(_cross_kernel, heads=heads, scale=LANES ** -0.5),
        out_shape=jax.ShapeDtypeStruct((batch * nq * tq, w), BF16),
        grid=(batch, nq),
        in_specs=[pl.BlockSpec((tq, w), q_map),
                  pl.BlockSpec((nm, w), lambda b, i: (b, 0)),
                  pl.BlockSpec((nm, w), lambda b, i: (b, 0))],
        out_specs=pl.BlockSpec((tq, w), lambda b, i: (b * nq + i, 0)),
        compiler_params=_cparams("parallel", "parallel"),
        name="cross_attn",
    )(q, mk, mv)


FFN_TILE = 256
FFN_SUB_ROWS = 256


def _silu_gate(cg, cv):
    return jax.nn.silu(cg) * cv


def _ffn_up_kernel(x_ref, w_ref, bg_ref, bv_ref, cwg_ref, cwv_ref, cbg_ref, cbv_ref,
                   h_ref, sg_ref, sv_ref, carry_g, carry_v, *, tiles_per_seq):
    i, j = pl.program_id(0), pl.program_id(1)
    tm, tn = h_ref.shape
    sub = min(tm, FFN_SUB_ROWS)
    rows = lax.broadcasted_iota(jnp.int32, (sub, tn), 0)

    @pl.when(i % tiles_per_seq == 0)
    def _():
        carry_g[j, SUBLANES - 2:, :] = bg_ref[...]
        carry_v[j, SUBLANES - 2:, :] = bv_ref[...]

    def conv(u, prev, cw_ref, cb_ref):
        p1, p2 = prev[SUBLANES - 1:], prev[SUBLANES - 2:SUBLANES - 1]
        s1 = jnp.where(rows == 0, p1, pltpu.roll(u, 1, axis=0))
        s2 = jnp.where(rows == 0, p2, jnp.where(rows == 1, p1, pltpu.roll(u, 2, axis=0)))
        cw = cw_ref[...]
        return cb_ref[...] + cw[0:1] * s2 + cw[1:2] * s1 + cw[2:3] * u, u[sub - SUBLANES:]

    prev_g, prev_v = carry_g[j], carry_v[j]
    nsub = tm // sub
    product = lambda s: jnp.dot(x_ref[s * sub:(s + 1) * sub, :], w_ref[...], preferred_element_type=F32)
    up_next = product(0)
    for s in range(nsub):
        rs = slice(s * sub, (s + 1) * sub)
        up = up_next
        if s + 1 < nsub:
            up_next = product(s + 1)
        cg, prev_g = conv(up[:, :tn], prev_g, cwg_ref, cbg_ref)
        cv, prev_v = conv(up[:, tn:], prev_v, cwv_ref, cbv_ref)
        h_ref[rs, :] = _silu_gate(cg, cv).astype(h_ref.dtype)
    carry_g[j], carry_v[j] = prev_g, prev_v
    sg_ref[...], sv_ref[...] = prev_g, prev_v


def _interleave_ffn_tiles(a):
    lead, f2 = a.shape[:-1], a.shape[-1]
    a = a.reshape(*lead, 2, f2 // (2 * FFN_TILE), FFN_TILE)
    return jnp.swapaxes(a, -3, -2).reshape(*lead, f2)


def _deinterleave_ffn_tiles(a):
    lead, f2 = a.shape[:-1], a.shape[-1]
    a = a.reshape(*lead, f2 // (2 * FFN_TILE), 2, FFN_TILE)
    return jnp.swapaxes(a, -3, -2).reshape(*lead, f2)


def _ffn_up_prompt(x, w_up_il, layer, buf, conv_w, conv_b, *, batch):
    bt, k = x.shape
    f2 = w_up_il.shape[2]
    f = f2 // 2
    t = bt // batch
    tm = _pick(t, (1024, 512, 256, 128))
    tps = t // tm
    nv = f // FFN_TILE
    half = lambda blk, fn: (pl.BlockSpec(blk, lambda i, j: fn(i, j, 0)), pl.BlockSpec(blk, lambda i, j: fn(i, j, nv)))
    w_spec = pl.BlockSpec((None, k, 2 * FFN_TILE), lambda i, j: (layer, 0, j))
    b_specs = half((None, CONV_WIDTH - 1, FFN_TILE), lambda i, j, o: (i // tps, 0, j + o))
    cw_specs = half((CONV_WIDTH, FFN_TILE), lambda i, j, o: (0, j + o))
    cb_specs = half((1, FFN_TILE), lambda i, j, o: (0, j + o))
    state_spec = pl.BlockSpec((None, SUBLANES, FFN_TILE), lambda i, j: (i, 0, j))
    h, tails_g, tails_v = pl.pallas_call(
        functools.partial(_ffn_up_kernel, tiles_per_seq=tps),
        out_shape=[jax.ShapeDtypeStruct((bt, f), BF16),
                   jax.ShapeDtypeStruct((bt // tm, SUBLANES, f), F32),
                   jax.ShapeDtypeStruct((bt // tm, SUBLANES, f), F32)],
        grid=(bt // tm, nv),
        in_specs=[pl.BlockSpec((tm, k), lambda i, j: (i, 0)), w_spec, *b_specs, *cw_specs, *cb_specs],
        out_specs=[pl.BlockSpec((tm, FFN_TILE), lambda i, j: (i, j)), state_spec, state_spec],
        scratch_shapes=[pltpu.VMEM((nv, SUBLANES, FFN_TILE), F32), pltpu.VMEM((nv, SUBLANES, FFN_TILE), F32)],
        compiler_params=_cparams("arbitrary", "arbitrary"),
        name="ffn_up_conv_p",
    )(x, w_up_il, buf, buf, conv_w, conv_w, conv_b.reshape(1, f2), conv_b.reshape(1, f2))
    return h, tails_g[tps - 1::tps], tails_v[tps - 1::tps]


def _conv_sample_kernel(x2g, x1g, x0g, x2v, x1v, x0v, wg_ref, wv_ref, cbg_ref, cbv_ref, o_ref):
    wg, wv = wg_ref[...], wv_ref[...]
    cg = cbg_ref[...] + wg[0:1] * x2g[...] + wg[1:2] * x1g[...] + wg[2:3] * x0g[...]
    cv = cbv_ref[...] + wv[0:1] * x2v[...] + wv[1:2] * x1v[...] + wv[2:3] * x0v[...]
    o_ref[...] = _silu_gate(cg, cv).astype(o_ref.dtype)


def _conv_gate_sample(hp, conv_w, conv_b):
    bd, t2, f2 = hp.shape
    t = t2 - (CONV_WIDTH - 1)
    f = f2 // 2
    nv = f // FFN_TILE
    shifted = [hp[:, s:s + t].reshape(bd * t, f2) for s in range(CONV_WIDTH)]
    xs = lambda off: pl.BlockSpec((bd * t, FFN_TILE), lambda j: (0, j + off))
    taps = lambda off: pl.BlockSpec((CONV_WIDTH, FFN_TILE), lambda j: (0, j + off))
    cb = lambda off: pl.BlockSpec((1, FFN_TILE), lambda j: (0, j + off))
    return pl.pallas_call(
        _conv_sample_kernel,
        out_shape=jax.ShapeDtypeStruct((bd * t, f), BF16),
        grid=(nv,),
        in_specs=[xs(0)] * 3 + [xs(nv)] * 3 + [taps(0), taps(nv), cb(0), cb(nv)],
        out_specs=pl.BlockSpec((bd * t, FFN_TILE), lambda j: (0, j)),
        compiler_params=_cparams("parallel"),
        name="conv_gate_sample",
    )(*shifted, *shifted, conv_w, conv_w, conv_b.reshape(1, f2), conv_b.reshape(1, f2))


def kernel(x_prompt, x_sample, cache_attn_k, cache_attn_v, cache_mem_k, cache_mem_v, state_ssm_re, state_ssm_im, state_ffn_conv, page_table, mem_prompt, norm_mix, norm_cross, norm_ffn, norm_final, ssm_w_in, ssm_a_re, ssm_a_im, ssm_log_dt, ssm_b_re, ssm_b_im, ssm_c_re, ssm_c_im, ssm_d, ssm_w_glu, attn_w_qkv, attn_lambda, attn_subln, attn_w_o, rel_bias, cross_w_q, cross_w_kv, cross_w_o, ffn_w_up, ffn_conv_w, ffn_conv_b, ffn_w_down):
    bp, seq, dm = x_prompt.shape
    bd, dec_t, _ = x_sample.shape
    depth = norm_mix.shape[0]
    page, nh, hv = cache_attn_k.shape[2:]
    n_pages = page_table.shape[1]
    n_mem, mem_heads, mem_hd = cache_mem_k.shape[2:]
    mw = mem_heads * mem_hd
    d_ff = ffn_w_down.shape[1]
    nb = rel_bias.shape[0]
    assert hv == LANES and mem_hd == LANES and d_ff % FFN_TILE == 0
    attn_scale = (hv // 2) ** -0.5

    xp = x_prompt.reshape(bp * seq, dm)
    xs = x_sample.reshape(bd * dec_t, dm)
    mem_b = mem_prompt.reshape(bp * n_mem, dm).astype(BF16)

    blk = min(256, seq)
    far = _far_distance(nb)
    assert blk + 1 >= far and seq % blk == 0
    kpos, qpos = np.arange(blk)[:, None], np.arange(blk)[None, :]
    diag = np.where(kpos > qpos, -1, _rel_bucket_np(qpos - kpos, nb))
    idx_prompt = np.concatenate([diag, _rel_bucket_np(blk + qpos - kpos, nb)], axis=0)
    past = n_pages * page
    assert page + 1 >= far
    tt =np.arange(SUBLANES)[:, None]
    kk = np.arange(page)[None, :]
    tab_far = np.full((SUBLANES, page), nb - 1)
    tab_last = _rel_bucket_np(past + tt - ((n_pages - 1) * page + kk), nb)
    tab_new = np.where((kk <= tt) & (kk < dec_t) & (tt < dec_t), _rel_bucket_np(tt - kk, nb), -1)
    idx_sample = np.concatenate([tab_far, tab_last, tab_new], axis=0).astype(np.int32)

    w_in, w_glu = ssm_w_in.astype(BF16), ssm_w_glu.astype(BF16)
    w_qkv, w_o = attn_w_qkv.astype(BF16), attn_w_o.astype(BF16)
    w_q, w_kv, w_co = cross_w_q.astype(BF16), cross_w_kv.astype(BF16), cross_w_o.astype(BF16)
    w_up, w_down = _interleave_ffn_tiles(ffn_w_up).astype(BF16), ffn_w_down.astype(BF16)

    outs = {k: [] for k in ("ssm_re_p", "ssm_im_p", "k_p", "v_p", "mk_p", "mv_p", "conv_p",
                            "ssm_re_s", "ssm_im_s", "k_s", "v_s", "conv_s")}
    for i in range(depth):
        j = i // 2
        hp = _rmsnorm(xp, norm_mix[i], BF16)
        hs = _rmsnorm(xs, norm_mix[i], BF16)
        if i % 2 == 0:
            prep = _ssm_prep(ssm_a_re[j], ssm_a_im[j], ssm_log_dt[j], ssm_b_re[j], ssm_b_im[j],
                             ssm_c_re[j], ssm_c_im[j])
            up_ = _mm(hp, w_in, j, name="ssm_in_p")
            us_ = _mm(hs, w_in, j, name="ssm_in_s")
            zp, hrp, hip = _ssm_mixer_prompt(up_, prep, ssm_d[j], bp)
            zs, hrs, his = _ssm_mixer_sample(us_, prep, ssm_d[j], state_ssm_re[j], state_ssm_im[j])
            xp = _glu_mm(zp, w_glu, j, xp)
            xs = _glu_mm(zs, w_glu, j, xs)
            outs["ssm_re_p"].append(hrp)
            outs["ssm_im_p"].append(hip)
            outs["ssm_re_s"].append(hrs)
            outs["ssm_im_s"].append(his)
        else:
            lam_init = 0.8 - 0.6 * math.exp(-0.3 * i)
            width = nh * hv
            qkv = lambda h, tag: (_mm(h, w_qkv, j, n=width, out_dtype=BF16, scale=attn_scale, name="q_" + tag),
                                  _mm(h, w_qkv, j, n=width, col_off=width, name="k_" + tag),
                                  _mm(h, w_qkv, j, n=width, col_off=2 * width, name="v_" + tag))
            qp, kp_, vp_ = qkv(hp, "p")
            qs, ks_, vs_ = qkv(hs, "s")
            bias_p = _bias_tables(rel_bias, idx_prompt).reshape(nh, 2, blk, blk)
            tabs_s = _bias_tables(rel_bias, idx_sample).reshape(nh, 3, SUBLANES, page)
            ap = _attn_prompt(qp, kp_, vp_, bias_p, attn_lambda[j], attn_subln[j],
                              batch=bp, lam_init=lam_init, blk=blk)
            as_ = _attn_sample(qs, ks_, vs_, cache_attn_k, cache_attn_v, j, page_table, tabs_s,
                               attn_lambda[j], attn_subln[j], lam_init=lam_init)
            xp = _mm(ap, w_o, j, res=xp, name="attn_o_p")
            xs = _mm(as_, w_o, j, res=xs, name="attn_o_s")
            outs["k_p"].append(kp_.reshape(bp, seq, nh, hv))
            outs["v_p"].append(vp_.reshape(bp, seq, nh, hv))
            outs["k_s"].append(ks_.reshape(bd, dec_t, nh, hv))
            outs["v_s"].append(vs_.reshape(bd, dec_t, nh, hv))

        hp = _rmsnorm(xp, norm_cross[i], BF16)
        hs = _rmsnorm(xs, norm_cross[i], BF16)
        mkp = _mm(mem_b, w_kv, i, n=mw, name="mem_k")
        mvp = _mm(mem_b, w_kv, i, n=mw, col_off=mw, name="mem_v")
        outs["mk_p"].append(mkp.reshape(bp, n_mem, mem_heads, mem_hd))
        outs["mv_p"].append(mvp.reshape(bp, n_mem, mem_heads, mem_hd))
        cq_p = _mm(hp, w_q, i, out_dtype=BF16, name="cross_q_p")
        cq_s = _mm(hs, w_q, i, out_dtype=BF16, name="cross_q_s")
        co_p = _cross_attn(cq_p, mkp, mvp, batch=bp, shared_q=False)
        co_all = _cross_attn(cq_s, cache_mem_k[i].reshape(bd * n_mem, mw), cache_mem_v[i].reshape(bd * n_mem, mw),
                             batch=bd, shared_q=True)
        co_all = co_all.reshape(bd, bd, dec_t, mw)
        co_s = jnp.stack([co_all[b, b] for b in range(bd)]).reshape(bd * dec_t, mw)
        xp = _mm(co_p, w_co, i, res=xp, name="cross_o_p")
        xs = _mm(co_s, w_co, i, res=xs, name="cross_o_s")

        hp = _rmsnorm(xp, norm_ffn[i], BF16)
        hs = _rmsnorm(xs, norm_ffn[i], BF16)
        buf0 = jnp.zeros((bp, CONV_WIDTH - 1, 2 * d_ff), F32)
        gp, tail_g, tail_v = _ffn_up_prompt(hp, w_up, i, buf0, ffn_conv_w[i], ffn_conv_b[i], batch=bp)
        up_s = _deinterleave_ffn_tiles(_mm(hs, w_up, i, name="ffn_up_s"))
        hp_s = jnp.concatenate([state_ffn_conv[i], up_s.reshape(bd, dec_t, 2 * d_ff)], axis=1)
        gs = _conv_gate_sample(hp_s, ffn_conv_w[i], ffn_conv_b[i])
        xp = _mm(gp, w_down, i, res=xp, name="ffn_down_p")
        xs = _mm(gs, w_down, i, res=xs, name="ffn_down_s")
        keep = CONV_WIDTH - 1
        assert seq >= keep
        outs["conv_p"].append(jnp.concatenate([tail_g[:, -keep:], tail_v[:, -keep:]], axis=-1))
        outs["conv_s"].append(hp_s[:, -keep:])

    y_prompt = _rmsnorm(xp, norm_final, F32).reshape(bp, seq, dm)
    y_sample = _rmsnorm(xs, norm_final, F32).reshape(bd, dec_t, dm)
    st = lambda k: jnp.stack(outs[k])
    return (y_prompt, y_sample, st("ssm_re_p"), st("ssm_im_p"), st("k_p"), st("v_p"),
            st("mk_p"), st("mv_p"), st("conv_p"), st("ssm_re_s"), st("ssm_im_s"),
            st("k_s"), st("v_s"), st("conv_s"))
```

```python
import functools
import math

import jax
import jax.numpy as jnp
import numpy as np
from jax import lax
from jax.experimental import pallas as pl
from jax.experimental.pallas import tpu as pltpu

F32 = jnp.float32
BF16 = jnp.bfloat16

SSM_GROUP = 16
MAX_DISTANCE = 128
RMS_EPS = 1e-6
SUBLN_EPS = 1e-5
CONV_WIDTH = 3

LANES = 128
SUBLANES = 8
VMEM_LIMIT_BYTES = 56 * 1024 * 1024

MASK_VALUE = -1e30
NT_DIMS = (((1,), (1,)), ((), ()))


def _cparams(*sem):
    return pltpu.CompilerParams(dimension_semantics=sem, vmem_limit_bytes=VMEM_LIMIT_BYTES)


def _pick(n, cands):
    for c in cands:
        if n % c == 0:
            return c
    return n


def _rmsnorm_kernel(x_ref, g_ref, o_ref):
    x = x_ref[...]
    ms = jnp.mean(x * x, axis=-1, keepdims=True)
    o_ref[...] = (x * lax.rsqrt(ms + RMS_EPS) * g_ref[...]).astype(o_ref.dtype)


def _rmsnorm(x, g, out_dtype):
    m, d = x.shape
    tm = _pick(m, (256,))
    return pl.pallas_call(
        _rmsnorm_kernel,
        out_shape=jax.ShapeDtypeStruct((m, d), out_dtype),
        grid=(m // tm,),
        in_specs=[pl.BlockSpec((tm, d), lambda i: (i, 0)),
                  pl.BlockSpec((1, d), lambda i: (0, 0))],
        out_specs=pl.BlockSpec((tm, d), lambda i: (i, 0)),
        compiler_params=_cparams("parallel"),
        name="rmsnorm",
    )(x, g.reshape(1, d))


def _mm_kernel(*refs, nk, scale, has_res):
    if has_res:
        x_ref, w_ref, r_ref, o_ref = refs[:4]
        rest = refs[4:]
    else:
        x_ref, w_ref, o_ref = refs[:3]
        r_ref = None
        rest = refs[3:]

    def finish(acc):
        if scale is not None:
            acc = acc * scale
        if r_ref is not None:
            acc = r_ref[...] + acc
        o_ref[...] = acc.astype(o_ref.dtype)

    part = jnp.dot(x_ref[...], w_ref[...], preferred_element_type=F32)
    if nk == 1:
        finish(part)
        return
    acc_ref, = rest
    k = pl.program_id(2)

    @pl.when(k == 0)
    def _():
        acc_ref[...] = part

    @pl.when(k > 0)
    def _():
        acc_ref[...] += part

    @pl.when(k == nk - 1)
    def _():
        finish(acc_ref[...])


MAX_K_BLOCK = 6144


def _mm_tiles(m, n, k):
    tm = _pick(m, (1024,))
    if m >= 1024:
        tn = _pick(n, (512, 256, 128))
    else:
        tn = _pick(n, (1024, 512, 256, 128))
    tk = k if k <= MAX_K_BLOCK else _pick(k, range(MAX_K_BLOCK - MAX_K_BLOCK % LANES, 0, -LANES))
    return tm, tn, tk


def _mm(x, w, layer, *, n=None, col_off=0, out_dtype=F32, res=None, scale=None, name="mm"):
    m, k = x.shape
    n = w.shape[2] if n is None else n
    tm, tn, tk = _mm_tiles(m, n, k)
    nk = k // tk
    assert col_off % tn == 0 and m % tm == 0 and n % tn == 0 and k % tk == 0
    cb = col_off // tn
    in_specs = [pl.BlockSpec((tm, tk), lambda i, j, kk: (i, kk)),
                pl.BlockSpec((None, tk, tn), lambda i, j, kk: (layer, kk, j + cb))]
    args = [x, w]
    if res is not None:
        in_specs.append(pl.BlockSpec((tm, tn), lambda i, j, kk: (i, j)))
        args.append(res)
    return pl.pallas_call(
        functools.partial(_mm_kernel, nk=nk, scale=scale, has_res=res is not None),
        out_shape=jax.ShapeDtypeStruct((m, n), out_dtype),
        grid=(m // tm, n // tn, nk),
        in_specs=in_specs,
        out_specs=pl.BlockSpec((tm, tn), lambda i, j, kk: (i, j)),
        scratch_shapes=[pltpu.VMEM((tm, tn), F32)] if nk > 1 else [],
        compiler_params=_cparams("parallel", "parallel", "arbitrary"),
        name=name,
    )(*args)


def _glu_kernel(x_ref, wa_ref, wb_ref, r_ref, o_ref):
    x = x_ref[...]
    a = jnp.dot(x, wa_ref[...], preferred_element_type=F32)
    b = jnp.dot(x, wb_ref[...], preferred_element_type=F32)
    o_ref[...] = r_ref[...] + a * jax.nn.sigmoid(b)


def _glu_mm(x, w, layer, res):
    m, k = x.shape
    d = w.shape[2] // 2
    tm = _pick(m, (1024,))
    tn = _pick(d, (512, 256, 128))
    nj = d // tn
    return pl.pallas_call(
        _glu_kernel,
        out_shape=jax.ShapeDtypeStruct((m, d), F32),
        grid=(m // tm, nj),
        in_specs=[pl.BlockSpec((tm, k), lambda i, j: (i, 0)),
                  pl.BlockSpec((None, k, tn), lambda i, j: (layer, 0, j)),
                  pl.BlockSpec((None, k, tn), lambda i, j: (layer, 0, j + nj)),
                  pl.BlockSpec((tm, tn), lambda i, j: (i, j))],
        out_specs=pl.BlockSpec((tm, tn), lambda i, j: (i, j)),
        compiler_params=_cparams("parallel", "parallel"),
        name="glu_mm",
    )(x, w, w, res)


def _ssm_discretise(ar, ai, log_dt):
    dt = jnp.exp(log_dt)
    mag = jnp.exp(ar * dt)
    return mag * jnp.cos(ai * dt), mag * jnp.sin(ai * dt)


def _ssm_prep_kernel(are_ref, aim_ref, ldt_ref, arc_ref, aic_ref, ldc_ref, btr_ref, bti_ref, ctr_ref, cti_ref,
                     lr_ref, li_ref, wb_ref, wc_ref):
    lr_ref[...], li_ref[...] = _ssm_discretise(are_ref[...], aim_ref[...], ldt_ref[...])
    ar, ai = arc_ref[...], aic_ref[...]
    lr, li = _ssm_discretise(ar, ai, ldc_ref[...])
    den = ar * ar + ai * ai
    nr = lr - 1.0
    kr = (nr * ar + li * ai) / den
    ki = (li * ar - nr * ai) / den
    btr, bti = btr_ref[...], bti_ref[...]
    rows = SUBLANES * SSM_GROUP
    b_re = (kr * btr - ki * bti).reshape(rows, LANES)
    b_im = (kr * bti + ki * btr).reshape(rows, LANES)
    c_re = ctr_ref[...].reshape(rows, LANES)
    c_im = -cti_ref[...].reshape(rows, LANES)
    row_group = lax.broadcasted_iota(jnp.int32, (rows, LANES), 0) // SSM_GROUP
    lane_half = lax.broadcasted_iota(jnp.int32, (rows, LANES), 1) // (LANES // 2)
    nblk = SUBLANES // 2
    for half, (bsrc, csrc) in enumerate(((b_re, c_re), (b_im, c_im))):
        for mblk in range(nblk):
            keep = row_group == 2 * mblk + lane_half
            sl = slice((half * nblk + mblk) * LANES, (half * nblk + mblk + 1) * LANES)
            wb_ref[:, sl] = jnp.where(keep, bsrc, 0.0).astype(wb_ref.dtype)
            wc_ref[:, sl] = jnp.where(keep, csrc, 0.0).astype(wc_ref.dtype)


def _ssm_prep(a_re, a_im, log_dt, b_re, b_im, c_re, c_im):
    g, p = a_re.shape
    assert 2 * p == LANES and g % SUBLANES == 0
    nt = g // SUBLANES
    width = SUBLANES * LANES
    dup = lambda a: jnp.concatenate([a, a], axis=-1)
    btr = dup(jnp.swapaxes(b_re, 1, 2))
    bti = dup(jnp.swapaxes(b_im, 1, 2))
    ldt = jnp.broadcast_to(log_dt[:, None], (g, LANES))
    on_rows = lambda a: jnp.broadcast_to(a[:, None, :], (g, SSM_GROUP, LANES))
    vec =pl.BlockSpec((SUBLANES, LANES), lambda j: (j, 0))
    cube = pl.BlockSpec((SUBLANES, SSM_GROUP, LANES), lambda j: (j, 0, 0))
    wide = pl.BlockSpec((None, SUBLANES * SSM_GROUP, width), lambda j: (j, 0, 0))
    return pl.pallas_call(
        _ssm_prep_kernel,
        out_shape=[jax.ShapeDtypeStruct((g, LANES), F32), jax.ShapeDtypeStruct((g, LANES), F32),
                   jax.ShapeDtypeStruct((nt, SUBLANES * SSM_GROUP, width), BF16),
                   jax.ShapeDtypeStruct((nt, SUBLANES * SSM_GROUP, width), BF16)],
        grid=(nt,),
        in_specs=[vec, vec, vec, cube, cube, cube, cube, cube, cube, cube],
        out_specs=[vec, vec, wide, wide],
        compiler_params=_cparams("parallel"),
        name="ssm_prep",
    )(dup(a_re), dup(a_im), ldt, on_rows(dup(a_re)), on_rows(dup(a_im)), on_rows(ldt),
      btr, bti, dup(c_re), dup(c_im))


SCAN_CHUNK = 128
SCAN_UNROLL = 4


def _ssm_scan_kernel(*refs, paired, steps):
    if paired:
        (u_ref, wb0_ref, wb1_ref, wc0_ref, wc1_ref, lr_ref, li_ref, d_ref, h0_ref,
         z_ref, hf_ref, bu_ref, hs_ref) = refs
    else:
        (u_ref, wb0_ref, wc0_ref, lr_ref, li_ref, d_ref, h0_ref,
         z_ref, hf_ref, bu_ref, hs_ref) = refs
    half = hs_ref.shape[1] // 2

    @pl.when(pl.program_id(1) == 0)
    def _():
        hs_ref[...] = h0_ref[...]

    u = u_ref[...]
    if paired:
        odd = lax.broadcasted_iota(jnp.int32, (u.shape[0], 1), 0) % 2 == 1
        ub = jnp.concatenate([jnp.where(odd, 0.0, u), jnp.where(odd, u, 0.0)], axis=1).astype(BF16)
        wb = jnp.concatenate([wb0_ref[...], wb1_ref[...]], axis=0)
    else:
        ub, wb = u.astype(BF16), wb0_ref[...]
    bu_ref[...] = jnp.dot(ub, wb, preferred_element_type=F32)

    lr, li = lr_ref[...], li_ref[...]

    def step(t, carry):
        hr, hi = carry
        row = pl.multiple_of(t * SUBLANES, SUBLANES)
        blk = bu_ref[pl.ds(row, SUBLANES), :]
        nr = hr * lr - hi * li + blk[:, :half]
        ni = hr * li + hi * lr + blk[:, half:]
        bu_ref[pl.ds(row, SUBLANES), :] = jnp.concatenate([nr, ni], axis=1)
        return nr, ni

    h0 = hs_ref[...]
    hr, hi = lax.fori_loop(0, steps, step, (h0[:, :half], h0[:, half:]), unroll=SCAN_UNROLL)
    hfin = jnp.concatenate([hr, hi], axis=1)
    hs_ref[...] = hfin
    hf_ref[...] = hfin

    hs = bu_ref[...].astype(BF16)
    if paired:
        wc = jnp.concatenate([wc0_ref[...], wc1_ref[...]], axis=0)
        y2 = lax.dot_general(hs, wc, NT_DIMS, preferred_element_type=F32)
        y = jnp.where(odd, y2[:, LANES:], y2[:, :LANES])
    else:
        y = lax.dot_general(hs, wc0_ref[...], NT_DIMS, preferred_element_type=F32)
    y = (y.reshape(steps, SUBLANES, LANES) + d_ref[...][None] * u.reshape(steps, SUBLANES, LANES))
    z_ref[...] = jax.nn.gelu(y).reshape(steps * SUBLANES, LANES).astype(z_ref.dtype)


def _ssm_scan(u_rows, wb, wc, lr_slab, li_slab, d_slab, h0, *, paired):
    rows, c = u_rows.shape
    t = rows // SUBLANES
    nj = c // LANES
    lt = _pick(t, (SCAN_CHUNK,))
    width = wb.shape[2]
    off = wb.shape[0] // 2
    u_spec = pl.BlockSpec((lt * SUBLANES, LANES), lambda j, s: (s, j))
    w_lo = pl.BlockSpec((None, LANES, width), lambda j, s: (j, 0, 0))
    w_hi = pl.BlockSpec((None, LANES, width), lambda j, s: (j + off, 0, 0))
    slab = lambda n: pl.BlockSpec((None, SUBLANES, n), lambda j, s: (j, 0, 0))
    if paired:
        in_specs = [u_spec, w_lo, w_hi, w_lo, w_hi]
        args = [u_rows, wb, wb, wc, wc]
    else:
        in_specs = [u_spec, w_lo, w_lo]
        args = [u_rows, wb, wc]
    in_specs += [slab(width // 2), slab(width // 2), slab(LANES), slab(width)]
    args += [lr_slab, li_slab, d_slab, h0]
    return pl.pallas_call(
        functools.partial(_ssm_scan_kernel, paired=paired, steps=lt),
        out_shape=[jax.ShapeDtypeStruct((rows, c), BF16),
                   jax.ShapeDtypeStruct((nj, SUBLANES, width), F32)],
        grid=(nj, t // lt),
        in_specs=in_specs,
        out_specs=[u_spec, slab(width)],
        scratch_shapes=[pltpu.VMEM((lt * SUBLANES, width), F32), pltpu.VMEM((SUBLANES, width), F32)],
        compiler_params=_cparams("parallel", "arbitrary"),
        name="ssm_scan",
    )(*args)


def _ssm_tiles(v, nt):
    return v[:, : LANES // 2].reshape(nt, SUBLANES * (LANES // 2))


def _ssm_mixer_prompt(u, prep, d, batch):
    lr, li, wb, wc = prep
    nt = wb.shape[0]
    bt, dm = u.shape
    t = bt // batch
    assert 2 * batch == SUBLANES and nt % 2 == 0
    hw = SUBLANES * (LANES // 2)
    u_rows = u.reshape(batch, t, 2, dm // 2).transpose(1, 0, 2, 3).reshape(t * SUBLANES, dm // 2)

    def pair_slab(tiles):
        n = tiles.shape[1]
        s = jnp.stack([tiles[: nt // 2], tiles[nt // 2:]], axis=1)
        return jnp.broadcast_to(s[:, None], (nt // 2, batch, 2, n)).reshape(nt // 2, SUBLANES, n)

    z_rows, hfin = _ssm_scan(u_rows, wb, wc, pair_slab(_ssm_tiles(lr, nt)), pair_slab(_ssm_tiles(li, nt)),
                             pair_slab(d.reshape(nt, LANES)), jnp.zeros((nt // 2, SUBLANES, 2 * hw), F32),
                             paired=True)
    z = z_rows.reshape(t, batch, 2, dm // 2).transpose(1, 0, 2, 3).reshape(bt, dm)

    def states(h):
        h = h.reshape(nt // 2, batch, 2, SUBLANES, LANES // 2).transpose(1, 2, 0, 3, 4)
        return h.reshape(batch, nt * SUBLANES, LANES // 2)

    return z, states(hfin[..., :hw]), states(hfin[..., hw:])


def _ssm_mixer_sample(u, prep, d, h0_re, h0_im):
    lr, li, wb, wc = prep
    nt = wb.shape[0]
    bd = h0_re.shape[0]
    bt, dm = u.shape
    t = bt // bd
    assert bd == SUBLANES
    hw = SUBLANES * (LANES // 2)
    u_rows = u.reshape(bd, t, dm).transpose(1, 0, 2).reshape(t * bd, dm)
    slab = lambda tiles: jnp.broadcast_to(tiles[:, None], (nt, SUBLANES, tiles.shape[1]))
    to_tiles = lambda h: h.reshape(bd, nt, hw).transpose(1, 0, 2)
    h0 = jnp.concatenate([to_tiles(h0_re), to_tiles(h0_im)], axis=-1)
    z_rows, hfin = _ssm_scan(u_rows, wb, wc, slab(_ssm_tiles(lr, nt)), slab(_ssm_tiles(li, nt)),
                             slab(d.reshape(nt, LANES)), h0, paired=False)
    z = z_rows.reshape(t, bd, dm).transpose(1, 0, 2).reshape(bt, dm)
    states = lambda h: h.transpose(1, 0, 2).reshape(bd, nt * SUBLANES, LANES // 2)
    return z, states(hfin[..., :hw]), states(hfin[..., hw:])


def _rel_bucket_np(dist, num_buckets):
    n = np.maximum(dist, 0)
    max_exact = num_buckets // 2
    nf = np.maximum(n, 1).astype(np.float32)
    large = max_exact + (np.log(nf / np.float32(max_exact)) / np.float32(math.log(MAX_DISTANCE / max_exact))
                         * np.float32(num_buckets - max_exact)).astype(np.int32)
    large = np.minimum(large, num_buckets - 1)
    return np.where(n < max_exact, n, large).astype(np.int32)


def _far_distance(num_buckets):
    b = _rel_bucket_np(np.arange(4 * MAX_DISTANCE), num_buckets)
    return int(np.max(np.nonzero(b != num_buckets - 1)[0])) + 1


def _bias_kernel(rb_ref, idx_ref, o_ref, *, num_buckets):
    h = pl.program_id(0)
    idx = idx_ref[...]
    acc = jnp.full(idx.shape, MASK_VALUE, F32)
    for b in range(num_buckets):
        acc = jnp.where(idx == b, rb_ref[b, h], acc)
    o_ref[...] = acc


def _bias_tables(rel_bias, idx):
    nb, nh = rel_bias.shape
    r, c = idx.shape
    return pl.pallas_call(
        functools.partial(_bias_kernel, num_buckets=nb),
        out_shape=jax.ShapeDtypeStruct((nh, r, c), F32),
        grid=(nh,),
        in_specs=[pl.BlockSpec(memory_space=pltpu.SMEM),
                  pl.BlockSpec((r, c), lambda h: (0, 0))],
        out_specs=pl.BlockSpec((None, r, c), lambda h: (h, 0, 0)),
        compiler_params=_cparams("parallel"),
        name="rel_bias_tables",
    )(rel_bias, jnp.asarray(idx))


def _diff_lambda(lam_ref, lam_init):
    lp = lam_ref[...]
    e1 = jnp.exp(jnp.sum(lp[0:1] * lp[1:2], keepdims=True))
    e2 = jnp.exp(jnp.sum(lp[2:3] * lp[3:4], keepdims=True))
    return e1 - e2 + lam_init


def _diff_merge(o1, o2, lam, subln, lam_init):
    d = o1 - lam * o2
    d = d * lax.rsqrt(jnp.mean(d * d, axis=-1, keepdims=True) + SUBLN_EPS) * subln
    return d * (1.0 - lam_init)


ATTN_HEADS_PER_STEP = 4


def _attn_prompt_kernel(lam_ref, subln_ref, q_ref, k_ref, v_ref, bias_ref, o_ref,
                        kb_ref, vt_ref, m_ref, l_ref, acc_ref, *, blk, lam_init):
    qi = pl.program_id(2)
    heads = kb_ref.shape[0]
    e = LANES

    @pl.when(qi == 0)
    def _():
        for h in range(heads):
            for j in range(kb_ref.shape[1]):
                rows, cols = slice(j * blk, (j + 1) * blk), slice(h * e, (h + 1) * e)
                kb_ref[h, j] = k_ref[rows, cols].astype(BF16)
                vt_ref[h, j] = v_ref[rows, cols].T.astype(BF16)

    feat = lax.broadcasted_iota(jnp.int32, (e, blk), 0)
    qts = []
    for h in range(heads):
        qt = q_ref[:, h * e:(h + 1) * e].astype(F32).T
        qts.append(jnp.concatenate([jnp.where(feat < e // 2, qt, 0.0), jnp.where(feat >= e // 2, qt, 0.0)],
                                   axis=1).astype(BF16))

    m_ref[...] = jnp.full(m_ref.shape, MASK_VALUE, F32)
    l_ref[...] = jnp.zeros(l_ref.shape, F32)
    acc_ref[...] = jnp.zeros(acc_ref.shape, F32)

    def attend(kb, tile):
        sts, ps, corrs = [], [], []
        for h in range(heads):
            sts.append(jnp.dot(kb_ref[h, kb], qts[h], preferred_element_type=F32))
        for h in range(heads):
            if tile is None:
                bias = bias_ref[h, 1, 0:1, blk - 1:blk]
            else:
                bt = bias_ref[h, tile]
                bias = jnp.concatenate([bt, bt], axis=1)
            st = sts[h] + bias
            m_prev = m_ref[h]
            m_new = jnp.maximum(m_prev, jnp.max(st, axis=0, keepdims=True))
            corr = jnp.exp(m_prev - m_new)
            p = jnp.exp(st - m_new)
            l_ref[h] = l_ref[h] * corr + jnp.sum(p, axis=0, keepdims=True)
            m_ref[h] = m_new
            ps.append(p.astype(BF16))
            corrs.append(corr)
        for h in range(heads):
            acc_ref[h] = acc_ref[h] * corrs[h] + jnp.dot(vt_ref[h, kb], ps[h], preferred_element_type=F32)

    def far_body(kb, carry):
        attend(kb, None)
        return carry

    lax.fori_loop(0, jnp.maximum(qi - 1, 0), far_body, 0)

    @pl.when(qi >= 1)
    def _():
        attend(qi - 1, 1)

    attend(qi, 0)

    lam = _diff_lambda(lam_ref, lam_init)
    for h in range(heads):
        ot = acc_ref[h] / l_ref[h]
        dt = ot[:, :blk] - lam * ot[:, blk:]
        dt = dt * lax.rsqrt(jnp.mean(dt * dt, axis=0, keepdims=True) + SUBLN_EPS)
        o_ref[:, h * e:(h + 1) * e] = (dt.T * subln_ref[...] * (1.0 - lam_init)).astype(o_ref.dtype)


def _attn_prompt(q, k, v, bias, lam_p, subln, *, batch, lam_init, blk):
    bs, he = q.shape
    s = bs // batch
    e = LANES
    nh = he // e
    nq = s // blk
    hps = ATTN_HEADS_PER_STEP if nh % ATTN_HEADS_PER_STEP == 0 else 1
    w = hps * e
    return pl.pallas_call(
        functools.partial(_attn_prompt_kernel, blk=blk, lam_init=lam_init),
        out_shape=jax.ShapeDtypeStruct((bs, he), BF16),
        grid=(batch, nh // hps, nq),
        in_specs=[pl.BlockSpec(lam_p.shape, lambda b, h, i: (0, 0)),
                  pl.BlockSpec((1, e), lambda b, h, i: (0, 0)),
                  pl.BlockSpec((blk, w), lambda b, h, i: (b * nq + i, h)),
                  pl.BlockSpec((s, w), lambda b, h, i: (b, h)),
                  pl.BlockSpec((s, w), lambda b, h, i: (b, h)),
                  pl.BlockSpec((hps, 2, blk, blk), lambda b, h, i: (h, 0, 0, 0))],
        out_specs=pl.BlockSpec((blk, w), lambda b, h, i: (b * nq + i, h)),
        scratch_shapes=[pltpu.VMEM((hps, nq, blk, e), BF16), pltpu.VMEM((hps, nq, e, blk), BF16),
                        pltpu.VMEM((hps, 1, 2 * blk), F32), pltpu.VMEM((hps, 1, 2 * blk), F32),
                        pltpu.VMEM((hps, e, 2 * blk), F32)],
        compiler_params=_cparams("parallel", "parallel", "arbitrary"),
        name="diff_attn_prompt",
    )(lam_p, subln.reshape(1, e), q, k, v, bias)


HEAD_GROUP = SUBLANES
DECODE_PAGES_PER_STEP = 4


def _attn_sample_kernel(pt_ref, lam_ref, subln_ref, q_ref, *refs, n_steps, pps, lam_init):
    ck_refs, cv_refs = refs[:pps], refs[pps:2 * pps]
    kn_ref, vn_ref, bias_far_ref, bias_ref, biasn_ref, o_ref, m_ref, l_ref, acc_ref = refs[2 * pps:]
    p = pl.program_id(1)
    ng = q_ref.shape[0]

    @pl.when(p == 0)
    def _():
        m_ref[...] = jnp.full(m_ref.shape, MASK_VALUE, F32)
        l_ref[...] = jnp.zeros(l_ref.shape, F32)
        acc_ref[...] = jnp.zeros(acc_ref.shape, F32)

    def attend(k_ref, v_ref, b_ref):
        keys = k_ref.shape[0]
        group = lambda ref, g: ref[:, g * HEAD_GROUP:(g + 1) * HEAD_GROUP, :].reshape(
            keys * HEAD_GROUP, LANES).astype(BF16)
        scores = [lax.dot_general(q_ref[g], group(k_ref, g), NT_DIMS, preferred_element_type=F32)
                  for g in range(ng)]
        probs, corrs = [], []
        for g in range(ng):
            s = scores[g] + b_ref[g]
            m_prev = m_ref[g]
            m_new = jnp.maximum(m_prev, jnp.max(s, axis=-1, keepdims=True))
            corr = jnp.exp(m_prev - m_new)
            pe = jnp.exp(s - m_new)
            l_ref[g] = l_ref[g] * corr + jnp.sum(pe, axis=-1, keepdims=True)
            m_ref[g] = m_new
            probs.append(pe.astype(BF16))
            corrs.append(corr)
        for g in range(ng):
            acc_ref[g] = acc_ref[g] * corrs[g] + jnp.dot(probs[g], group(v_ref, g), preferred_element_type=F32)

    @pl.when(p < n_steps)
    def _():
        for r in range(pps):
            attend(ck_refs[r], cv_refs[r], bias_ref if r == pps - 1 else bias_far_ref)

    @pl.when(p == n_steps)
    def _():
        attend(kn_ref, vn_ref, biasn_ref)
        o = acc_ref[...] / l_ref[...]
        half = o.shape[1] // 2
        lam = _diff_lambda(lam_ref, lam_init)
        o_ref[...] = _diff_merge(o[:, :half], o[:, half:], lam, subln_ref[...], lam_init).astype(o_ref.dtype)


def _attn_sample(q, k_new, v_new, cache_k, cache_v, layer, page_table, tabs, lam_p, subln, *, lam_init):
    bd, n_pages = page_table.shape
    page, nh, e = cache_k.shape[2:]
    t = q.shape[0] // bd
    assert e == LANES and nh % HEAD_GROUP == 0 and t <= SUBLANES
    ng = nh // HEAD_GROUP
    hh = e // 2
    rows = 2 * HEAD_GROUP * t

    q5 = q.reshape(bd, t, ng, HEAD_GROUP, e).transpose(0, 2, 3, 1, 4)
    lane = jnp.arange(e) < hh
    qm = jnp.stack([jnp.where(lane, q5, 0), jnp.where(lane, 0, q5)], axis=2)
    qm = qm.reshape(bd, ng, rows, e)

    same = np.arange(HEAD_GROUP)[:, None] == np.arange(HEAD_GROUP)[None, :]

    def expand(tab, keys):
        tb = tab[:, :t, :keys].reshape(ng, HEAD_GROUP, t, keys)
        full = jnp.where(same[None, :, None, None, :], tb[..., None], MASK_VALUE)
        full = jnp.broadcast_to(full[:, None], (ng, 2, HEAD_GROUP, t, keys, HEAD_GROUP))
        return full.reshape(ng, rows, keys * HEAD_GROUP)

    bias_pages = jnp.stack([expand(tabs[:, 0], page), expand(tabs[:, 1], page)])
    bias_new = expand(tabs[:, 2], SUBLANES)

    pad = lambda a: jnp.pad(a.reshape(bd, t, nh, e), ((0, 0), (0, SUBLANES - t), (0, 0), (0, 0)))
    pps = _pick(n_pages, (DECODE_PAGES_PER_STEP, 2, 1))
    n_steps = n_pages // pps
    page_maps = [lambda b, p, pt, r=r: (layer, pt[b, jnp.minimum(p, n_steps - 1) * pps + r], 0, 0, 0)
                 for r in range(pps)]
    page_specs = [pl.BlockSpec((None, None, page, nh, e), pm) for pm in page_maps]
    const2 = lambda b, p, pt: (0, 0)
    per_b = lambda b, p, pt: (b, 0, 0, 0)
    bias_spec = lambda sel: pl.BlockSpec((None, ng, rows, page * HEAD_GROUP), lambda b, p, pt: (sel(p), 0, 0, 0))
    out = pl.pallas_call(
        functools.partial(_attn_sample_kernel, n_steps=n_steps, pps=pps, lam_init=lam_init),
        out_shape=jax.ShapeDtypeStruct((bd, ng, rows // 2, e), BF16),
        grid_spec=pltpu.PrefetchScalarGridSpec(
            num_scalar_prefetch=1,
            grid=(bd, n_steps + 1),
            in_specs=[pl.BlockSpec(lam_p.shape, const2),
                      pl.BlockSpec((1, e), const2),
                      pl.BlockSpec((None, ng, rows, e), per_b),
                      *page_specs, *page_specs,
                      pl.BlockSpec((None, SUBLANES, nh, e), per_b),
                      pl.BlockSpec((None, SUBLANES, nh, e), per_b),
                      bias_spec(lambda p: 0),
                      bias_spec(lambda p: jnp.where(p >= n_steps - 1, 1, 0)),
                      pl.BlockSpec((ng, rows, SUBLANES * HEAD_GROUP), lambda b, p, pt: (0, 0, 0))],
            out_specs=pl.BlockSpec((None, ng, rows // 2, e), per_b),
            scratch_shapes=[pltpu.VMEM((ng, rows, 1), F32), pltpu.VMEM((ng, rows, 1), F32),
                            pltpu.VMEM((ng, rows, e), F32)]),
        compiler_params=_cparams("parallel", "arbitrary"),
        name="diff_attn_sample",
    )(page_table, lam_p, subln.reshape(1, e), qm, *([cache_k] * pps), *([cache_v] * pps),
      pad(k_new), pad(v_new), bias_pages, bias_pages, bias_new)
    return out.reshape(bd, ng, HEAD_GROUP, t, e).transpose(0, 3, 1, 2, 4).reshape(bd * t, nh * e)


def _cross_kernel(q_ref, k_ref, v_ref, o_ref, *, heads, scale):
    q = q_ref[...]
    k = k_ref[...].astype(BF16)
    v = v_ref[...].astype(BF16)
    outs = []
    for h in range(heads):
        sl = slice(h * LANES, (h + 1) * LANES)
        s = lax.dot_general(q[:, sl], k[:, sl], NT_DIMS, preferred_element_type=F32) * scale
        e = jnp.exp(s - jnp.max(s, axis=-1, keepdims=True))
        o = jnp.dot(e.astype(BF16), v[:, sl], preferred_element_type=F32)
        outs.append(o / jnp.sum(e, axis=-1, keepdims=True))
    o_ref[...] = jnp.concatenate(outs, axis=1).astype(o_ref.dtype)


def _cross_attn(q, mk, mv, *, batch, shared_q):
    rows, w = q.shape
    nm = mk.shape[0] // batch
    heads = w // LANES
    if shared_q:
        tq, nq = rows, 1
        q_map = lambda b, i: (0, 0)
    else:
        per = rows // batch
        tq = _pick(per, (512, 256, 128))
        nq = per // tq
        q_map = lambda b, i: (b * nq + i, 0)
    return pl.pallas_call(
        functools.partial(_cross_kernel, heads=heads, scale=LANES ** -0.5),
        out_shape=jax.ShapeDtypeStruct((batch * nq * tq, w), BF16),
        grid=(batch, nq),
        in_specs=[pl.BlockSpec((tq, w), q_map),
                  pl.BlockSpec((nm, w), lambda b, i: (b, 0)),
                  pl.BlockSpec((nm, w), lambda b, i: (b, 0))],
        out_specs=pl.BlockSpec((tq, w), lambda b, i: (b * nq + i, 0)),
        compiler_params=_cparams("parallel", "parallel"),
        name="cross_attn",
    )(q, mk, mv)


FFN_TILE = 256
FFN_SUB_ROWS = 256


def _ffn_sub_tiles(tm):
    sub = min(tm, FFN_SUB_ROWS)
    return [slice(a, a + sub) for a in range(0, tm, sub)]


def _silu_gate(cg, cv):
    return jax.nn.silu(cg) * cv


def _ffn_up_kernel(x_ref, w_ref, bg_ref, bv_ref, cwg_ref, cwv_ref, cbg_ref, cbv_ref,
                   h_ref, sg_ref, sv_ref, carry_g, carry_v, *, tiles_per_seq):
    i, j = pl.program_id(0), pl.program_id(1)
    tm, tn = h_ref.shape

    @pl.when(i % tiles_per_seq == 0)
    def _():
        carry_g[j, SUBLANES - 2:, :] = bg_ref[...]
        carry_v[j, SUBLANES - 2:, :] = bv_ref[...]

    def conv(u, prev, cw_ref, cb_ref):
        rows = lax.broadcasted_iota(jnp.int32, u.shape, 0)
        p1, p2 = prev[SUBLANES - 1:], prev[SUBLANES - 2:SUBLANES - 1]
        s1 = jnp.where(rows == 0, p1, pltpu.roll(u, 1, axis=0))
        s2 = jnp.where(rows == 0, p2, jnp.where(rows == 1, p1, pltpu.roll(u, 2, axis=0)))
        cw = cw_ref[...]
        return cb_ref[...] + cw[0:1] * s2 + cw[1:2] * s1 + cw[2:3] * u, u[u.shape[0] - SUBLANES:]

    bounds = _ffn_sub_tiles(tm)
    prev_g, prev_v = carry_g[j], carry_v[j]
    product = lambda rs: jnp.dot(x_ref[rs, :], w_ref[...], preferred_element_type=F32)
    up_next = product(bounds[0])
    for s, rs in enumerate(bounds):
        up = up_next
        if s + 1 < len(bounds):
            up_next = product(bounds[s + 1])
        cg, prev_g = conv(up[:, :tn], prev_g, cwg_ref, cbg_ref)
        cv, prev_v = conv(up[:, tn:], prev_v, cwv_ref, cbv_ref)
        h_ref[rs, :] = _silu_gate(cg, cv).astype(h_ref.dtype)
    carry_g[j], carry_v[j] = prev_g, prev_v
    sg_ref[...], sv_ref[...] = prev_g, prev_v


def _ffn_weight_kernel(g_ref, v_ref, o_ref):
    o_ref[:, :FFN_TILE] = g_ref[...].astype(o_ref.dtype)
    o_ref[:, FFN_TILE:] = v_ref[...].astype(o_ref.dtype)


def _interleave_ffn_weight(w):
    nl, k, f2 = w.shape
    nv = f2 // (2 * FFN_TILE)
    tk = _pick(k, (2048, 1024, 512, 256, 128))
    return pl.pallas_call(
        _ffn_weight_kernel,
        out_shape=jax.ShapeDtypeStruct((nl, k, f2), BF16),
        grid=(nl, k // tk, nv),
        in_specs=[pl.BlockSpec((None, tk, FFN_TILE), lambda a, i, j: (a, i, j)),
                  pl.BlockSpec((None, tk, FFN_TILE), lambda a, i, j: (a, i, j + nv))],
        out_specs=pl.BlockSpec((None, tk, 2 * FFN_TILE), lambda a, i, j: (a, i, j)),
        compiler_params=_cparams("parallel", "parallel", "parallel"),
        name="ffn_weight_interleave",
    )(w, w)


def _deinterleave_ffn_tiles(a):
    lead, f2 = a.shape[:-1], a.shape[-1]
    a = a.reshape(*lead, f2 // (2 * FFN_TILE), 2, FFN_TILE)
    return jnp.swapaxes(a, -3, -2).reshape(*lead, f2)


def _ffn_up_prompt(x, w_up_il, layer, buf, conv_w, conv_b, *, batch):
    bt, k = x.shape
    f2 = w_up_il.shape[2]
    f = f2 // 2
    t = bt // batch
    tm = _pick(t, (1024, 512, 256, 128))
    tps = t // tm
    nv = f // FFN_TILE
    half = lambda blk, fn: (pl.BlockSpec(blk, lambda i, j: fn(i, j, 0)), pl.BlockSpec(blk, lambda i, j: fn(i, j, nv)))
    w_spec = pl.BlockSpec((None, k, 2 * FFN_TILE), lambda i, j: (layer, 0, j))
    b_specs = half((None, CONV_WIDTH - 1, FFN_TILE), lambda i, j, o: (i // tps, 0, j + o))
    cw_specs = half((CONV_WIDTH, FFN_TILE), lambda i, j, o: (0, j + o))
    cb_specs = half((1, FFN_TILE), lambda i, j, o: (0, j + o))
    state_spec = pl.BlockSpec((None, SUBLANES, FFN_TILE), lambda i, j: (i, 0, j))
    h, tails_g, tails_v = pl.pallas_call(
        functools.partial(_ffn_up_kernel, tiles_per_seq=tps),
        out_shape=[jax.ShapeDtypeStruct((bt, f), BF16),
                   jax.ShapeDtypeStruct((bt // tm, SUBLANES, f), F32),
                   jax.ShapeDtypeStruct((bt // tm, SUBLANES, f), F32)],
        grid=(bt // tm, nv),
        in_specs=[pl.BlockSpec((tm, k), lambda i, j: (i, 0)), w_spec, *b_specs, *cw_specs, *cb_specs],
        out_specs=[pl.BlockSpec((tm, FFN_TILE), lambda i, j: (i, j)), state_spec, state_spec],
        scratch_shapes=[pltpu.VMEM((nv, SUBLANES, FFN_TILE), F32), pltpu.VMEM((nv, SUBLANES, FFN_TILE), F32)],
        compiler_params=_cparams("arbitrary", "arbitrary"),
        name="ffn_up_conv_p",
    )(x, w_up_il, buf, buf, conv_w, conv_w, conv_b.reshape(1, f2), conv_b.reshape(1, f2))
    return h, tails_g[tps - 1::tps], tails_v[tps - 1::tps]


def _conv_sample_kernel(x2g, x1g, x0g, x2v, x1v, x0v, wg_ref, wv_ref, cbg_ref, cbv_ref, o_ref):
    wg, wv = wg_ref[...], wv_ref[...]
    cg = cbg_ref[...] + wg[0:1] * x2g[...] + wg[1:2] * x1g[...] + wg[2:3] * x0g[...]
    cv = cbv_ref[...] + wv[0:1] * x2v[...] + wv[1:2] * x1v[...] + wv[2:3] * x0v[...]
    o_ref[...] = _silu_gate(cg, cv).astype(o_ref.dtype)


def _conv_gate_sample(hp, conv_w, conv_b):
    bd, t2, f2 = hp.shape
    t = t2 - (CONV_WIDTH - 1)
    f = f2 // 2
    nv = f // FFN_TILE
    shifted = [hp[:, s:s + t].reshape(bd * t, f2) for s in range(CONV_WIDTH)]
    xs = lambda off: pl.BlockSpec((bd * t, FFN_TILE), lambda j: (0, j + off))
    taps = lambda off: pl.BlockSpec((CONV_WIDTH, FFN_TILE), lambda j: (0, j + off))
    cb = lambda off: pl.BlockSpec((1, FFN_TILE), lambda j: (0, j + off))
    return pl.pallas_call(
        _conv_sample_kernel,
        out_shape=jax.ShapeDtypeStruct((bd * t, f), BF16),
        grid=(nv,),
        in_specs=[xs(0)] * 3 + [xs(nv)] * 3 + [taps(0), taps(nv), cb(0), cb(nv)],
        out_specs=pl.BlockSpec((bd * t, FFN_TILE), lambda j: (0, j)),
        compiler_params=_cparams("parallel"),
        name="conv_gate_sample",
    )(*shifted, *shifted, conv_w, conv_w, conv_b.reshape(1, f2), conv_b.reshape(1, f2))


def kernel(x_prompt, x_sample, cache_attn_k, cache_attn_v, cache_mem_k, cache_mem_v, state_ssm_re, state_ssm_im, state_ffn_conv, page_table, mem_prompt, norm_mix, norm_cross, norm_ffn, norm_final, ssm_w_in, ssm_a_re, ssm_a_im, ssm_log_dt, ssm_b_re, ssm_b_im, ssm_c_re, ssm_c_im, ssm_d, ssm_w_glu, attn_w_qkv, attn_lambda, attn_subln, attn_w_o, rel_bias, cross_w_q, cross_w_kv, cross_w_o, ffn_w_up, ffn_conv_w, ffn_conv_b, ffn_w_down):
    bp, seq, dm = x_prompt.shape
    bd, dec_t, _ = x_sample.shape
    depth = norm_mix.shape[0]
    page, nh, hv = cache_attn_k.shape[2:]
    n_pages = page_table.shape[1]
    n_mem, mem_heads, mem_hd = cache_mem_k.shape[2:]
    mw = mem_heads * mem_hd
    d_ff = ffn_w_down.shape[1]
    nb = rel_bias.shape[0]
    assert hv == LANES and mem_hd == LANES and d_ff % FFN_TILE == 0
    attn_scale = (hv // 2) ** -0.5

    xp = x_prompt.reshape(bp * seq, dm)
    xs = x_sample.reshape(bd * dec_t, dm)
    mem_b = mem_prompt.reshape(bp * n_mem, dm).astype(BF16)

    blk = min(256, seq)
    far = _far_distance(nb)
    assert blk + 1 >= far and seq % blk == 0
    kpos, qpos = np.arange(blk)[:, None], np.arange(blk)[None, :]
    diag = np.where(kpos > qpos, -1, _rel_bucket_np(qpos - kpos, nb))
    idx_prompt = np.concatenate([diag, _rel_bucket_np(blk + qpos - kpos, nb)], axis=0)
    past = n_pages * page
    assert page + 1 >= far
    tt =np.arange(SUBLANES)[:, None]
    kk = np.arange(page)[None, :]
    tab_far = np.full((SUBLANES, page), nb - 1)
    tab_last = _rel_bucket_np(past + tt - ((n_pages - 1) * page + kk), nb)
    tab_new = np.where((kk <= tt) & (kk < dec_t) & (tt < dec_t), _rel_bucket_np(tt - kk, nb), -1)
    idx_sample = np.concatenate([tab_far, tab_last, tab_new], axis=0).astype(np.int32)

    w_in, w_glu = ssm_w_in.astype(BF16), ssm_w_glu.astype(BF16)
    w_qkv, w_o = attn_w_qkv.astype(BF16), attn_w_o.astype(BF16)
    w_q, w_kv, w_co = cross_w_q.astype(BF16), cross_w_kv.astype(BF16), cross_w_o.astype(BF16)
    w_up, w_down = _interleave_ffn_weight(ffn_w_up), ffn_w_down.astype(BF16)

    outs = {k: [] for k in ("ssm_re_p", "ssm_im_p", "k_p", "v_p", "mk_p", "mv_p", "conv_p",
                            "ssm_re_s", "ssm_im_s", "k_s", "v_s", "conv_s")}
    for i in range(depth):
        j = i // 2
        hp = _rmsnorm(xp, norm_mix[i], BF16)
        hs = _rmsnorm(xs, norm_mix[i], BF16)
        if i % 2 == 0:
            prep = _ssm_prep(ssm_a_re[j], ssm_a_im[j], ssm_log_dt[j], ssm_b_re[j], ssm_b_im[j],
                             ssm_c_re[j], ssm_c_im[j])
            up_ = _mm(hp, w_in, j, name="ssm_in_p")
            us_ = _mm(hs, w_in, j, name="ssm_in_s")
            zp, hrp, hip = _ssm_mixer_prompt(up_, prep, ssm_d[j], bp)
            zs, hrs, his = _ssm_mixer_sample(us_, prep, ssm_d[j], state_ssm_re[j], state_ssm_im[j])
            xp = _glu_mm(zp, w_glu, j, xp)
            xs = _glu_mm(zs, w_glu, j, xs)
            outs["ssm_re_p"].append(hrp)
            outs["ssm_im_p"].append(hip)
            outs["ssm_re_s"].append(hrs)
            outs["ssm_im_s"].append(his)
        else:
            lam_init = 0.8 - 0.6 * math.exp(-0.3 * i)
            width = nh * hv
            qkv = lambda h, tag: (_mm(h, w_qkv, j, n=width, out_dtype=BF16, scale=attn_scale, name="q_" + tag),
                                  _mm(h, w_qkv, j, n=width, col_off=width, name="k_" + tag),
                                  _mm(h, w_qkv, j, n=width, col_off=2 * width, name="v_" + tag))
            qp, kp_, vp_ = qkv(hp, "p")
            qs, ks_, vs_ = qkv(hs, "s")
            bias_p = _bias_tables(rel_bias, idx_prompt).reshape(nh, 2, blk, blk)
            tabs_s = _bias_tables(rel_bias, idx_sample).reshape(nh, 3, SUBLANES, page)
            ap = _attn_prompt(qp, kp_, vp_, bias_p, attn_lambda[j], attn_subln[j],
                              batch=bp, lam_init=lam_init, blk=blk)
            as_ = _attn_sample(qs, ks_, vs_, cache_attn_k, cache_attn_v, j, page_table, tabs_s,
                               attn_lambda[j], attn_subln[j], lam_init=lam_init)
            xp = _mm(ap, w_o, j, res=xp, name="attn_o_p")
            xs = _mm(as_, w_o, j, res=xs, name="attn_o_s")
            outs["k_p"].append(kp_.reshape(bp, seq, nh, hv))
            outs["v_p"].append(vp_.reshape(bp, seq, nh, hv))
            outs["k_s"].append(ks_.reshape(bd, dec_t, nh, hv))
            outs["v_s"].append(vs_.reshape(bd, dec_t, nh, hv))

        hp = _rmsnorm(xp, norm_cross[i], BF16)
        hs = _rmsnorm(xs, norm_cross[i], BF16)
        mkp = _mm(mem_b, w_kv, i, n=mw, name="mem_k")
        mvp = _mm(mem_b, w_kv, i, n=mw, col_off=mw, name="mem_v")
        outs["mk_p"].append(mkp.reshape(bp, n_mem, mem_heads, mem_hd))
        outs["mv_p"].append(mvp.reshape(bp, n_mem, mem_heads, mem_hd))
        cq_p = _mm(hp, w_q, i, out_dtype=BF16, name="cross_q_p")
        cq_s = _mm(hs, w_q, i, out_dtype=BF16, name="cross_q_s")
        co_p = _cross_attn(cq_p, mkp, mvp, batch=bp, shared_q=False)
        co_all = _cross_attn(cq_s, cache_mem_k[i].reshape(bd * n_mem, mw), cache_mem_v[i].reshape(bd * n_mem, mw),
                             batch=bd, shared_q=True)
        co_all = co_all.reshape(bd, bd, dec_t, mw)
        co_s = jnp.stack([co_all[b, b] for b in range(bd)]).reshape(bd * dec_t, mw)
        xp = _mm(co_p, w_co, i, res=xp, name="cross_o_p")
        xs = _mm(co_s, w_co, i, res=xs, name="cross_o_s")

        hp = _rmsnorm(xp, norm_ffn[i], BF16)
        hs = _rmsnorm(xs, norm_ffn[i], BF16)
        buf0 = jnp.zeros((bp, CONV_WIDTH - 1, 2 * d_ff), F32)
        gp, tail_g, tail_v = _ffn_up_prompt(hp, w_up, i, buf0, ffn_conv_w[i], ffn_conv_b[i], batch=bp)
        up_s = _deinterleave_ffn_tiles(_mm(hs, w_up, i, name="ffn_up_s"))
        hp_s = jnp.concatenate([state_ffn_conv[i], up_s.reshape(bd, dec_t, 2 * d_ff)], axis=1)
        gs = _conv_gate_sample(hp_s, ffn_conv_w[i], ffn_conv_b[i])
        xp = _mm(gp, w_down, i, res=xp, name="ffn_down_p")
        xs = _mm(gs, w_down, i, res=xs, name="ffn_down_s")
        keep = CONV_WIDTH - 1
        assert seq >= keep
        outs["conv_p"].append(jnp.concatenate([tail_g[:, -keep:], tail_v[:, -keep:]], axis=-1))
        outs["conv_s"].append(hp_s[:, -keep:])

    y_prompt = _rmsnorm(xp, norm_final, F32).reshape(bp, seq, dm)
    y_sample = _rmsnorm(xs, norm_final, F32).reshape(bd, dec_t, dm)
    st = lambda k: jnp.stack(outs[k])
    return (y_prompt, y_sample, st("ssm_re_p"), st("ssm_im_p"), st("k_p"), st("v_p"),
            st("mk_p"), st("mv_p"), st("conv_p"), st("ssm_re_s"), st("ssm_im_s"),
            st("k_s"), st("v_s"), st("conv_s"))
```

```python
import functools
import math

import jax
import jax.numpy as jnp
import numpy as np
from jax import lax
from jax.experimental import pallas as pl
from jax.experimental.pallas import tpu as pltpu

F32 = jnp.float32
BF16 = jnp.bfloat16

SSM_GROUP = 16
MAX_DISTANCE = 128
RMS_EPS = 1e-6
SUBLN_EPS = 1e-5
CONV_WIDTH = 3

LANES = 128
SUBLANES = 8
VMEM_LIMIT_BYTES = 56 * 1024 * 1024

MASK_VALUE = -1e30
NT_DIMS = (((1,), (1,)), ((), ()))


def _cparams(*sem):
    return pltpu.CompilerParams(dimension_semantics=sem, vmem_limit_bytes=VMEM_LIMIT_BYTES)


def _pick(n, cands):
    for c in cands:
        if n % c == 0:
            return c
    return n


def _rmsnorm_kernel(x_ref, g_ref, o_ref):
    x = x_ref[...]
    ms = jnp.mean(x * x, axis=-1, keepdims=True)
    o_ref[...] = (x * lax.rsqrt(ms + RMS_EPS) * g_ref[...]).astype(o_ref.dtype)


def _rmsnorm(x, g, out_dtype):
    m, d = x.shape
    tm = _pick(m, (256,))
    return pl.pallas_call(
        _rmsnorm_kernel,
        out_shape=jax.ShapeDtypeStruct((m, d), out_dtype),
        grid=(m // tm,),
        in_specs=[pl.BlockSpec((tm, d), lambda i: (i, 0)),
                  pl.BlockSpec((1, d), lambda i: (0, 0))],
        out_specs=pl.BlockSpec((tm, d), lambda i: (i, 0)),
        compiler_params=_cparams("parallel"),
        name="rmsnorm",
    )(x, g.reshape(1, d))


def _mm_kernel(*refs, nk, scale, has_res):
    if has_res:
        x_ref, w_ref, r_ref, o_ref = refs[:4]
        rest = refs[4:]
    else:
        x_ref, w_ref, o_ref = refs[:3]
        r_ref = None
        rest = refs[3:]

    def finish(acc):
        if scale is not None:
            acc = acc * scale
        if r_ref is not None:
            acc = r_ref[...] + acc
        o_ref[...] = acc.astype(o_ref.dtype)

    part = jnp.dot(x_ref[...], w_ref[...], preferred_element_type=F32)
    if nk == 1:
        finish(part)
        return
    acc_ref, = rest
    k = pl.program_id(2)

    @pl.when(k == 0)
    def _():
        acc_ref[...] = part

    @pl.when(k > 0)
    def _():
        acc_ref[...] += part

    @pl.when(k == nk - 1)
    def _():
        finish(acc_ref[...])


MAX_K_BLOCK = 6144


def _mm_tiles(m, n, k):
    tm = _pick(m, (1024,))
    if m >= 1024:
        tn = _pick(n, (512, 256, 128))
    else:
        tn = _pick(n, (1024, 512, 256, 128))
    tk = k if k <= MAX_K_BLOCK else _pick(k, range(MAX_K_BLOCK - MAX_K_BLOCK % LANES, 0, -LANES))
    return tm, tn, tk


def _mm(x, w, layer, *, n=None, col_off=0, out_dtype=F32, res=None, scale=None, name="mm"):
    m, k = x.shape
    n = w.shape[2] if n is None else n
    tm, tn, tk = _mm_tiles(m, n, k)
    nk = k // tk
    assert col_off % tn == 0 and m % tm == 0 and n % tn == 0 and k % tk == 0
    cb = col_off // tn
    in_specs = [pl.BlockSpec((tm, tk), lambda i, j, kk: (i, kk)),
                pl.BlockSpec((None, tk, tn), lambda i, j, kk: (layer, kk, j + cb))]
    args = [x, w]
    if res is not None:
        in_specs.append(pl.BlockSpec((tm, tn), lambda i, j, kk: (i, j)))
        args.append(res)
    return pl.pallas_call(
        functools.partial(_mm_kernel, nk=nk, scale=scale, has_res=res is not None),
        out_shape=jax.ShapeDtypeStruct((m, n), out_dtype),
        grid=(m // tm, n // tn, nk),
        in_specs=in_specs,
        out_specs=pl.BlockSpec((tm, tn), lambda i, j, kk: (i, j)),
        scratch_shapes=[pltpu.VMEM((tm, tn), F32)] if nk > 1 else [],
        compiler_params=_cparams("parallel", "parallel", "arbitrary"),
        name=name,
    )(*args)


def _glu_kernel(x_ref, wa_ref, wb_ref, r_ref, o_ref):
    x = x_ref[...]
    a = jnp.dot(x, wa_ref[...], preferred_element_type=F32)
    b = jnp.dot(x, wb_ref[...], preferred_element_type=F32)
    o_ref[...] = r_ref[...] + a * jax.nn.sigmoid(b)


def _glu_mm(x, w, layer, res):
    m, k = x.shape
    d = w.shape[2] // 2
    tm = _pick(m, (1024,))
    tn = _pick(d, (512, 256, 128))
    nj = d // tn
    return pl.pallas_call(
        _glu_kernel,
        out_shape=jax.ShapeDtypeStruct((m, d), F32),
        grid=(m // tm, nj),
        in_specs=[pl.BlockSpec((tm, k), lambda i, j: (i, 0)),
                  pl.BlockSpec((None, k, tn), lambda i, j: (layer, 0, j)),
                  pl.BlockSpec((None, k, tn), lambda i, j: (layer, 0, j + nj)),
                  pl.BlockSpec((tm, tn), lambda i, j: (i, j))],
        out_specs=pl.BlockSpec((tm, tn), lambda i, j: (i, j)),
        compiler_params=_cparams("parallel", "parallel"),
        name="glu_mm",
    )(x, w, w, res)


def _ssm_discretise(ar, ai, log_dt):
    dt = jnp.exp(log_dt)
    mag = jnp.exp(ar * dt)
    return mag * jnp.cos(ai * dt), mag * jnp.sin(ai * dt)


def _ssm_prep_kernel(are_ref, aim_ref, ldt_ref, arc_ref, aic_ref, ldc_ref, btr_ref, bti_ref, ctr_ref, cti_ref,
                     lr_ref, li_ref, wb_ref, wc_ref):
    lr_ref[...], li_ref[...] = _ssm_discretise(are_ref[...], aim_ref[...], ldt_ref[...])
    ar, ai = arc_ref[...], aic_ref[...]
    lr, li = _ssm_discretise(ar, ai, ldc_ref[...])
    den = ar * ar + ai * ai
    nr = lr - 1.0
    kr = (nr * ar + li * ai) / den
    ki = (li * ar - nr * ai) / den
    btr, bti = btr_ref[...], bti_ref[...]
    rows = SUBLANES * SSM_GROUP
    b_re = (kr * btr - ki * bti).reshape(rows, LANES)
    b_im = (kr * bti + ki * btr).reshape(rows, LANES)
    c_re = ctr_ref[...].reshape(rows, LANES)
    c_im = -cti_ref[...].reshape(rows, LANES)
    row_group = lax.broadcasted_iota(jnp.int32, (rows, LANES), 0) // SSM_GROUP
    lane_half = lax.broadcasted_iota(jnp.int32, (rows, LANES), 1) // (LANES // 2)
    nblk = SUBLANES // 2
    for half, (bsrc, csrc) in enumerate(((b_re, c_re), (b_im, c_im))):
        for mblk in range(nblk):
            keep = row_group == 2 * mblk + lane_half
            sl = slice((half * nblk + mblk) * LANES, (half * nblk + mblk + 1) * LANES)
            wb_ref[:, sl] = jnp.where(keep, bsrc, 0.0).astype(wb_ref.dtype)
            wc_ref[:, sl] = jnp.where(keep, csrc, 0.0).astype(wc_ref.dtype)


def _ssm_prep(a_re, a_im, log_dt, b_re, b_im, c_re, c_im):
    g, p = a_re.shape
    assert 2 * p == LANES and g % SUBLANES == 0
    nt = g // SUBLANES
    width = SUBLANES * LANES
    dup = lambda a: jnp.concatenate([a, a], axis=-1)
    btr = dup(jnp.swapaxes(b_re, 1, 2))
    bti = dup(jnp.swapaxes(b_im, 1, 2))
    ldt = jnp.broadcast_to(log_dt[:, None], (g, LANES))
    on_rows = lambda a: jnp.broadcast_to(a[:, None, :], (g, SSM_GROUP, LANES))
    vec =pl.BlockSpec((SUBLANES, LANES), lambda j: (j, 0))
    cube = pl.BlockSpec((SUBLANES, SSM_GROUP, LANES), lambda j: (j, 0, 0))
    wide = pl.BlockSpec((None, SUBLANES * SSM_GROUP, width), lambda j: (j, 0, 0))
    return pl.pallas_call(
        _ssm_prep_kernel,
        out_shape=[jax.ShapeDtypeStruct((g, LANES), F32), jax.ShapeDtypeStruct((g, LANES), F32),
                   jax.ShapeDtypeStruct((nt, SUBLANES * SSM_GROUP, width), BF16),
                   jax.ShapeDtypeStruct((nt, SUBLANES * SSM_GROUP, width), BF16)],
        grid=(nt,),
        in_specs=[vec, vec, vec, cube, cube, cube, cube, cube, cube, cube],
        out_specs=[vec, vec, wide, wide],
        compiler_params=_cparams("parallel"),
        name="ssm_prep",
    )(dup(a_re), dup(a_im), ldt, on_rows(dup(a_re)), on_rows(dup(a_im)), on_rows(ldt),
      btr, bti, dup(c_re), dup(c_im))


SCAN_CHUNK = 128
SCAN_UNROLL = 4


def _weave_rows(a, b):
    n2 = a.shape[0] // SUBLANES
    a3, b3 = a.reshape(n2, SUBLANES, LANES), b.reshape(n2, SUBLANES, LANES)
    low = lax.broadcasted_iota(jnp.int32, a3.shape, 1) < SUBLANES // 2
    even = jnp.where(low, a3, pltpu.roll(b3, SUBLANES // 2, axis=1))
    odd = jnp.where(low, pltpu.roll(a3, SUBLANES // 2, axis=1), b3)
    return jnp.stack([even, odd], axis=1).reshape(2 * n2 * SUBLANES, LANES)


def _unweave_rows(x):
    n2 = x.shape[0] // (2 * SUBLANES)
    x4 = x.reshape(n2, 2, SUBLANES, LANES)
    even, odd = x4[:, 0], x4[:, 1]
    low = lax.broadcasted_iota(jnp.int32, even.shape, 1) < SUBLANES // 2
    a3 = jnp.where(low, even, pltpu.roll(odd, SUBLANES // 2, axis=1))
    b3 = jnp.where(low, pltpu.roll(even, SUBLANES // 2, axis=1), odd)
    return a3.reshape(n2 * SUBLANES, LANES), b3.reshape(n2 * SUBLANES, LANES)


def _ssm_scan_kernel(*refs, paired, steps):
    if paired:
        (ua_ref, ub_ref, wb0_ref, wb1_ref, wc0_ref, wc1_ref, lr_ref, li_ref, d_ref, h0_ref,
         za_ref, zb_ref, hf_ref, bu_ref, hs_ref) = refs
    else:
        (u_ref, wb0_ref, wc0_ref, lr_ref, li_ref, d_ref, h0_ref,
         z_ref, hf_ref, bu_ref, hs_ref) = refs
    half = hs_ref.shape[1] // 2

    @pl.when(pl.program_id(1) == 0)
    def _():
        hs_ref[...] = h0_ref[...]

    if paired:
        u = _weave_rows(ua_ref[...], ub_ref[...])
        odd = lax.broadcasted_iota(jnp.int32, (u.shape[0], 1), 0) % SUBLANES >= SUBLANES // 2
        ub = jnp.concatenate([jnp.where(odd, 0.0, u), jnp.where(odd, u, 0.0)], axis=1).astype(BF16)
        wb = jnp.concatenate([wb0_ref[...], wb1_ref[...]], axis=0)
    else:
        u = u_ref[...]
        ub, wb = u.astype(BF16), wb0_ref[...]
    bu_ref[...] = jnp.dot(ub, wb, preferred_element_type=F32)

    lr, li = lr_ref[...], li_ref[...]

    def step(t, carry):
        hr, hi = carry
        row = pl.multiple_of(t * SUBLANES, SUBLANES)
        blk = bu_ref[pl.ds(row, SUBLANES), :]
        nr = hr * lr - hi * li + blk[:, :half]
        ni = hr * li + hi * lr + blk[:, half:]
        bu_ref[pl.ds(row, SUBLANES), :] = jnp.concatenate([nr, ni], axis=1)
        return nr, ni

    h0 = hs_ref[...]
    hr, hi = lax.fori_loop(0, steps, step, (h0[:, :half], h0[:, half:]), unroll=SCAN_UNROLL)
    hfin = jnp.concatenate([hr, hi], axis=1)
    hs_ref[...] = hfin
    hf_ref[...] = hfin

    hs = bu_ref[...].astype(BF16)
    if paired:
        wc = jnp.concatenate([wc0_ref[...], wc1_ref[...]], axis=0)
        y2 = lax.dot_general(hs, wc, NT_DIMS, preferred_element_type=F32)
        y = jnp.where(odd, y2[:, LANES:], y2[:, :LANES])
    else:
        y = lax.dot_general(hs, wc0_ref[...], NT_DIMS, preferred_element_type=F32)
    y = (y.reshape(steps, SUBLANES, LANES) + d_ref[...][None] * u.reshape(steps, SUBLANES, LANES))
    z = jax.nn.gelu(y).reshape(steps * SUBLANES, LANES)
    if paired:
        za, zb = _unweave_rows(z)
        za_ref[...] = za.astype(za_ref.dtype)
        zb_ref[...] = zb.astype(zb_ref.dtype)
    else:
        z_ref[...] = z.astype(z_ref.dtype)


def _ssm_scan(u_rows, wb, wc, lr_slab, li_slab, d_slab, h0, *, paired):
    rows, c = u_rows.shape
    width = wb.shape[2]
    off = wb.shape[0] // 2
    seqs = SUBLANES // 2 if paired else SUBLANES
    t = rows // seqs
    nj = off if paired else c // LANES
    lt = _pick(t, (SCAN_CHUNK,))
    u_lo = pl.BlockSpec((lt * seqs, LANES), lambda j, s: (s, j))
    u_hi = pl.BlockSpec((lt * seqs, LANES), lambda j, s: (s, j + off))
    w_lo = pl.BlockSpec((None, LANES, width), lambda j, s: (j, 0, 0))
    w_hi = pl.BlockSpec((None, LANES, width), lambda j, s: (j + off, 0, 0))
    slab = lambda n: pl.BlockSpec((None, SUBLANES, n), lambda j, s: (j, 0, 0))
    if paired:
        in_specs = [u_lo, u_hi, w_lo, w_hi, w_lo, w_hi]
        args = [u_rows, u_rows, wb, wb, wc, wc]
        z_shapes = [jax.ShapeDtypeStruct((rows, c // 2), BF16)] * 2
        z_specs = [u_lo, u_lo]
    else:
        in_specs = [u_lo, w_lo, w_lo]
        args = [u_rows, wb, wc]
        z_shapes = [jax.ShapeDtypeStruct((rows, c), BF16)]
        z_specs = [u_lo]
    in_specs += [slab(width // 2), slab(width // 2), slab(LANES), slab(width)]
    args += [lr_slab, li_slab, d_slab, h0]
    return pl.pallas_call(
        functools.partial(_ssm_scan_kernel, paired=paired, steps=lt),
        out_shape=[*z_shapes, jax.ShapeDtypeStruct((nj, SUBLANES, width), F32)],
        grid=(nj, t // lt),
        in_specs=in_specs,
        out_specs=[*z_specs, slab(width)],
        scratch_shapes=[pltpu.VMEM((lt * SUBLANES, width), F32), pltpu.VMEM((SUBLANES, width), F32)],
        compiler_params=_cparams("parallel", "arbitrary"),
        name="ssm_scan",
    )(*args)


def _ssm_tiles(v, nt):
    return v[:, : LANES // 2].reshape(nt, SUBLANES * (LANES // 2))


def _ssm_mixer_prompt(u, prep, d, batch):
    lr, li, wb, wc = prep
    nt = wb.shape[0]
    bt, dm = u.shape
    t = bt // batch
    assert 2 * batch == SUBLANES and nt % 2 == 0
    hw = SUBLANES * (LANES // 2)
    u_rows = u.reshape(batch, t, dm).transpose(1, 0, 2).reshape(t * batch, dm)

    def pair_slab(tiles):
        n = tiles.shape[1]
        s = jnp.stack([tiles[: nt // 2], tiles[nt // 2:]], axis=1)
        return jnp.broadcast_to(s[:, :, None], (nt // 2, 2, batch, n)).reshape(nt // 2, SUBLANES, n)

    z_lo, z_hi, hfin = _ssm_scan(u_rows, wb, wc, pair_slab(_ssm_tiles(lr, nt)), pair_slab(_ssm_tiles(li, nt)),
                                 pair_slab(d.reshape(nt, LANES)), jnp.zeros((nt // 2, SUBLANES, 2 * hw), F32),
                                 paired=True)
    z = jnp.concatenate([z_lo, z_hi], axis=-1).reshape(t, batch, dm).transpose(1, 0, 2).reshape(bt, dm)

    def states(h):
        h = h.reshape(nt // 2, 2, batch, SUBLANES, LANES // 2).transpose(2, 1, 0, 3, 4)
        return h.reshape(batch, nt * SUBLANES, LANES // 2)

    return z, states(hfin[..., :hw]), states(hfin[..., hw:])


def _ssm_mixer_sample(u, prep, d, h0_re, h0_im):
    lr, li, wb, wc = prep
    nt = wb.shape[0]
    bd = h0_re.shape[0]
    bt, dm = u.shape
    t = bt // bd
    assert bd == SUBLANES
    hw = SUBLANES * (LANES // 2)
    u_rows = u.reshape(bd, t, dm).transpose(1, 0, 2).reshape(t * bd, dm)
    slab = lambda tiles: jnp.broadcast_to(tiles[:, None], (nt, SUBLANES, tiles.shape[1]))
    to_tiles = lambda h: h.reshape(bd, nt, hw).transpose(1, 0, 2)
    h0 = jnp.concatenate([to_tiles(h0_re), to_tiles(h0_im)], axis=-1)
    z_rows, hfin = _ssm_scan(u_rows, wb, wc, slab(_ssm_tiles(lr, nt)), slab(_ssm_tiles(li, nt)),
                             slab(d.reshape(nt, LANES)), h0, paired=False)
    z = z_rows.reshape(t, bd, dm).transpose(1, 0, 2).reshape(bt, dm)
    states = lambda h: h.transpose(1, 0, 2).reshape(bd, nt * SUBLANES, LANES // 2)
    return z, states(hfin[..., :hw]), states(hfin[..., hw:])


def _rel_bucket_np(dist, num_buckets):
    n = np.maximum(dist, 0)
    max_exact = num_buckets // 2
    nf = np.maximum(n, 1).astype(np.float32)
    large = max_exact + (np.log(nf / np.float32(max_exact)) / np.float32(math.log(MAX_DISTANCE / max_exact))
                         * np.float32(num_buckets - max_exact)).astype(np.int32)
    large = np.minimum(large, num_buckets - 1)
    return np.where(n < max_exact, n, large).astype(np.int32)


def _far_distance(num_buckets):
    b = _rel_bucket_np(np.arange(4 * MAX_DISTANCE), num_buckets)
    return int(np.max(np.nonzero(b != num_buckets - 1)[0])) + 1


def _bias_kernel(rb_ref, idx_ref, o_ref, *, num_buckets):
    h = pl.program_id(0)
    idx = idx_ref[...]
    acc = jnp.full(idx.shape, MASK_VALUE, F32)
    for b in range(num_buckets):
        acc = jnp.where(idx == b, rb_ref[b, h], acc)
    o_ref[...] = acc


def _bias_tables(rel_bias, idx):
    nb, nh = rel_bias.shape
    r, c = idx.shape
    return pl.pallas_call(
        functools.partial(_bias_kernel, num_buckets=nb),
        out_shape=jax.ShapeDtypeStruct((nh, r, c), F32),
        grid=(nh,),
        in_specs=[pl.BlockSpec(memory_space=pltpu.SMEM),
                  pl.BlockSpec((r, c), lambda h: (0, 0))],
        out_specs=pl.BlockSpec((None, r, c), lambda h: (h, 0, 0)),
        compiler_params=_cparams("parallel"),
        name="rel_bias_tables",
    )(rel_bias, jnp.asarray(idx))


def _diff_lambda(lam_ref, lam_init):
    lp = lam_ref[...]
    e1 = jnp.exp(jnp.sum(lp[0:1] * lp[1:2], keepdims=True))
    e2 = jnp.exp(jnp.sum(lp[2:3] * lp[3:4], keepdims=True))
    return e1 - e2 + lam_init


def _diff_merge(o1, o2, lam, subln, lam_init):
    d = o1 - lam * o2
    d = d * lax.rsqrt(jnp.mean(d * d, axis=-1, keepdims=True) + SUBLN_EPS) * subln
    return d * (1.0 - lam_init)


ATTN_HEADS_PER_STEP = 4


def _attn_prompt_kernel(lam_ref, subln_ref, q_ref, k_ref, v_ref, bias_ref, o_ref,
                        kb_ref, vt_ref, m_ref, l_ref, acc_ref, *, blk, lam_init):
    qi = pl.program_id(2)
    heads = kb_ref.shape[0]
    e = LANES

    @pl.when(qi == 0)
    def _():
        for h in range(heads):
            for j in range(kb_ref.shape[1]):
                rows, cols = slice(j * blk, (j + 1) * blk), slice(h * e, (h + 1) * e)
                kb_ref[h, j] = k_ref[rows, cols].astype(BF16)
                vt_ref[h, j] = v_ref[rows, cols].T.astype(BF16)

    feat = lax.broadcasted_iota(jnp.int32, (e, blk), 0)
    qts = []
    for h in range(heads):
        qt = q_ref[:, h * e:(h + 1) * e].astype(F32).T
        qts.append(jnp.concatenate([jnp.where(feat < e // 2, qt, 0.0), jnp.where(feat >= e // 2, qt, 0.0)],
                                   axis=1).astype(BF16))

    m_ref[...] = jnp.full(m_ref.shape, MASK_VALUE, F32)
    l_ref[...] = jnp.zeros(l_ref.shape, F32)
    acc_ref[...] = jnp.zeros(acc_ref.shape, F32)

    def attend(kbs, tile):
        sts, ps, corrs = [], [], []
        for h in range(heads):
            keys = jnp.concatenate([kb_ref[h, kb] for kb in kbs], axis=0)
            sts.append(jnp.dot(keys, qts[h], preferred_element_type=F32))
        for h in range(heads):
            if tile is None:
                bias = bias_ref[h, 1, 0:1, blk - 1:blk]
            else:
                bt = bias_ref[h, tile]
                bias = jnp.concatenate([bt, bt], axis=1)
            st = sts[h] + bias
            m_prev = m_ref[h]
            m_new = jnp.maximum(m_prev, jnp.max(st, axis=0, keepdims=True))
            corr = jnp.exp(m_prev - m_new)
            p = jnp.exp(st - m_new)
            l_ref[h] = l_ref[h] * corr + jnp.sum(p, axis=0, keepdims=True)
            m_ref[h] = m_new
            ps.append(p.astype(BF16))
            corrs.append(corr)
        for h in range(heads):
            vals = jnp.concatenate([vt_ref[h, kb] for kb in kbs], axis=1)
            acc_ref[h] = acc_ref[h] * corrs[h] + jnp.dot(vals, ps[h], preferred_element_type=F32)

    n_far = jnp.maximum(qi - 1, 0)

    def far_body(i, carry):
        attend([2 * i, 2 * i + 1], None)
        return carry

    lax.fori_loop(0, n_far // 2, far_body, 0)

    @pl.when(n_far % 2 == 1)
    def _():
        attend([n_far - 1], None)

    @pl.when(qi >= 1)
    def _():
        attend([qi - 1], 1)

    attend([qi], 0)

    lam = _diff_lambda(lam_ref, lam_init)
    for h in range(heads):
        ot = acc_ref[h] / l_ref[h]
        dt = ot[:, :blk] - lam * ot[:, blk:]
        dt = dt * lax.rsqrt(jnp.mean(dt * dt, axis=0, keepdims=True) + SUBLN_EPS)
        o_ref[:, h * e:(h + 1) * e] = (dt.T * subln_ref[...] * (1.0 - lam_init)).astype(o_ref.dtype)


def _attn_prompt(q, k, v, bias, lam_p, subln, *, batch, lam_init, blk):
    bs, he = q.shape
    s = bs // batch
    e = LANES
    nh = he // e
    nq = s // blk
    hps = ATTN_HEADS_PER_STEP if nh % ATTN_HEADS_PER_STEP == 0 else 1
    w = hps * e
    return pl.pallas_call(
        functools.partial(_attn_prompt_kernel, blk=blk, lam_init=lam_init),
        out_shape=jax.ShapeDtypeStruct((bs, he), BF16),
        grid=(batch, nh // hps, nq),
        in_specs=[pl.BlockSpec(lam_p.shape, lambda b, h, i: (0, 0)),
                  pl.BlockSpec((1, e), lambda b, h, i: (0, 0)),
                  pl.BlockSpec((blk, w), lambda b, h, i: (b * nq + i, h)),
                  pl.BlockSpec((s, w), lambda b, h, i: (b, h)),
                  pl.BlockSpec((s, w), lambda b, h, i: (b, h)),
                  pl.BlockSpec((hps, 2, blk, blk), lambda b, h, i: (h, 0, 0, 0))],
        out_specs=pl.BlockSpec((blk, w), lambda b, h, i: (b * nq + i, h)),
        scratch_shapes=[pltpu.VMEM((hps, nq, blk, e), BF16), pltpu.VMEM((hps, nq, e, blk), BF16),
                        pltpu.VMEM((hps, 1, 2 * blk), F32), pltpu.VMEM((hps, 1, 2 * blk), F32),
                        pltpu.VMEM((hps, e, 2 * blk), F32)],
        compiler_params=_cparams("parallel", "parallel", "arbitrary"),
        name="diff_attn_prompt",
    )(lam_p, subln.reshape(1, e), q, k, v, bias)


HEAD_GROUP = SUBLANES
DECODE_PAGES_PER_STEP = 4


def _attn_sample_kernel(pt_ref, lam_ref, subln_ref, q_ref, *refs, n_steps, pps, lam_init):
    ck_refs, cv_refs = refs[:pps], refs[pps:2 * pps]
    kn_ref, vn_ref, bias_far_ref, bias_ref, biasn_ref, o_ref, m_ref, l_ref, acc_ref = refs[2 * pps:]
    p = pl.program_id(1)
    ng = q_ref.shape[0]

    @pl.when(p == 0)
    def _():
        m_ref[...] = jnp.full(m_ref.shape, MASK_VALUE, F32)
        l_ref[...] = jnp.zeros(l_ref.shape, F32)
        acc_ref[...] = jnp.zeros(acc_ref.shape, F32)

    def attend(k_ref, v_ref, b_ref):
        keys = k_ref.shape[0]
        group = lambda ref, g: ref[:, g * HEAD_GROUP:(g + 1) * HEAD_GROUP, :].reshape(
            keys * HEAD_GROUP, LANES).astype(BF16)
        scores = [lax.dot_general(q_ref[g], group(k_ref, g), NT_DIMS, preferred_element_type=F32)
                  for g in range(ng)]
        probs, corrs = [], []
        for g in range(ng):
            s = scores[g] + b_ref[g]
            m_prev = m_ref[g]
            m_new = jnp.maximum(m_prev, jnp.max(s, axis=-1, keepdims=True))
            corr = jnp.exp(m_prev - m_new)
            pe = jnp.exp(s - m_new)
            l_ref[g] = l_ref[g] * corr + jnp.sum(pe, axis=-1, keepdims=True)
            m_ref[g] = m_new
            probs.append(pe.astype(BF16))
            corrs.append(corr)
        for g in range(ng):
            acc_ref[g] = acc_ref[g] * corrs[g] + jnp.dot(probs[g], group(v_ref, g), preferred_element_type=F32)

    @pl.when(p < n_steps)
    def _():
        for r in range(pps):
            attend(ck_refs[r], cv_refs[r], bias_ref if r == pps - 1 else bias_far_ref)

    @pl.when(p == n_steps)
    def _():
        attend(kn_ref, vn_ref, biasn_ref)
        o = acc_ref[...] / l_ref[...]
        half = o.shape[1] // 2
        lam = _diff_lambda(lam_ref, lam_init)
        o_ref[...] = _diff_merge(o[:, :half], o[:, half:], lam, subln_ref[...], lam_init).astype(o_ref.dtype)


def _attn_sample(q, k_new, v_new, cache_k, cache_v, layer, page_table, tabs, lam_p, subln, *, lam_init):
    bd, n_pages = page_table.shape
    page, nh, e = cache_k.shape[2:]
    t = q.shape[0] // bd
    assert e == LANES and nh % HEAD_GROUP == 0 and t <= SUBLANES
    ng = nh // HEAD_GROUP
    hh = e // 2
    rows = 2 * HEAD_GROUP * t

    q5 = q.reshape(bd, t, ng, HEAD_GROUP, e).transpose(0, 2, 3, 1, 4)
    lane = jnp.arange(e) < hh
    qm = jnp.stack([jnp.where(lane, q5, 0), jnp.where(lane, 0, q5)], axis=2)
    qm = qm.reshape(bd, ng, rows, e)

    same = np.arange(HEAD_GROUP)[:, None] == np.arange(HEAD_GROUP)[None, :]

    def expand(tab, keys):
        tb = tab[:, :t, :keys].reshape(ng, HEAD_GROUP, t, keys)
        full = jnp.where(same[None, :, None, None, :], tb[..., None], MASK_VALUE)
        full = jnp.broadcast_to(full[:, None], (ng, 2, HEAD_GROUP, t, keys, HEAD_GROUP))
        return full.reshape(ng, rows, keys * HEAD_GROUP)

    bias_pages = jnp.stack([expand(tabs[:, 0], page), expand(tabs[:, 1], page)])
    bias_new = expand(tabs[:, 2], SUBLANES)

    pad = lambda a: jnp.pad(a.reshape(bd, t, nh, e), ((0, 0), (0, SUBLANES - t), (0, 0), (0, 0)))
    pps = _pick(n_pages, (DECODE_PAGES_PER_STEP, 2, 1))
    n_steps = n_pages // pps
    page_maps = [lambda b, p, pt, r=r: (layer, pt[b, jnp.minimum(p, n_steps - 1) * pps + r], 0, 0, 0)
                 for r in range(pps)]
    page_specs = [pl.BlockSpec((None, None, page, nh, e), pm) for pm in page_maps]
    const2 = lambda b, p, pt: (0, 0)
    per_b = lambda b, p, pt: (b, 0, 0, 0)
    bias_spec = lambda sel: pl.BlockSpec((None, ng, rows, page * HEAD_GROUP), lambda b, p, pt: (sel(p), 0, 0, 0))
    out = pl.pallas_call(
        functools.partial(_attn_sample_kernel, n_steps=n_steps, pps=pps, lam_init=lam_init),
        out_shape=jax.ShapeDtypeStruct((bd, ng, rows // 2, e), BF16),
        grid_spec=pltpu.PrefetchScalarGridSpec(
            num_scalar_prefetch=1,
            grid=(bd, n_steps + 1),
            in_specs=[pl.BlockSpec(lam_p.shape, const2),
                      pl.BlockSpec((1, e), const2),
                      pl.BlockSpec((None, ng, rows, e), per_b),
                      *page_specs, *page_specs,
                      pl.BlockSpec((None, SUBLANES, nh, e), per_b),
                      pl.BlockSpec((None, SUBLANES, nh, e), per_b),
                      bias_spec(lambda p: 0),
                      bias_spec(lambda p: jnp.where(p >= n_steps - 1, 1, 0)),
                      pl.BlockSpec((ng, rows, SUBLANES * HEAD_GROUP), lambda b, p, pt: (0, 0, 0))],
            out_specs=pl.BlockSpec((None, ng, rows // 2, e), per_b),
            scratch_shapes=[pltpu.VMEM((ng, rows, 1), F32), pltpu.VMEM((ng, rows, 1), F32),
                            pltpu.VMEM((ng, rows, e), F32)]),
        compiler_params=_cparams("parallel", "arbitrary"),
        name="diff_attn_sample",
    )(page_table, lam_p, subln.reshape(1, e), qm, *([cache_k] * pps), *([cache_v] * pps),
      pad(k_new), pad(v_new), bias_pages, bias_pages, bias_new)
    return out.reshape(bd, ng, HEAD_GROUP, t, e).transpose(0, 3, 1, 2, 4).reshape(bd * t, nh * e)


def _cross_kernel(q_ref, k_ref, v_ref, o_ref, *, heads, scale):
    q = q_ref[...]
    k = k_ref[...].astype(BF16)
    v = v_ref[...].astype(BF16)
    outs = []
    for h in range(heads):
        sl = slice(h * LANES, (h + 1) * LANES)
        s = lax.dot_general(q[:, sl], k[:, sl], NT_DIMS, preferred_element_type=F32) * scale
        e = jnp.exp(s - jnp.max(s, axis=-1, keepdims=True))
        o = jnp.dot(e.astype(BF16), v[:, sl], preferred_element_type=F32)
        outs.append(o / jnp.sum(e, axis=-1, keepdims=True))
    o_ref[...] = jnp.concatenate(outs, axis=1).astype(o_ref.dtype)


def _cross_attn(q, mk, mv, *, batch, shared_q):
    rows, w = q.shape
    nm = mk.shape[0] // batch
    heads = w // LANES
    if shared_q:
        tq, nq = rows, 1
        q_map = lambda b, i: (0, 0)
    else:
        per = rows // batch
        tq = _pick(per, (512, 256, 128))
        nq = per // tq
        q_map = lambda b, i: (b * nq + i, 0)
    return pl.pallas_call(
        functools.partial(_cross_kernel, heads=heads, scale=LANES ** -0.5),
        out_shape=jax.ShapeDtypeStruct((batch * nq * tq, w), BF16),
        grid=(batch, nq),
        in_specs=[pl.BlockSpec((tq, w), q_map),
                  pl.BlockSpec((nm, w), lambda b, i: (b, 0)),
                  pl.BlockSpec((nm, w), lambda b, i: (b, 0))],
        out_specs=pl.BlockSpec((tq, w), lambda b, i: (b * nq + i, 0)),
        compiler_params=_cparams("parallel", "parallel"),
        name="cross_attn",
    )(q, mk, mv)


FFN_TILE = 256
FFN_SUB_ROWS = 256


def _ffn_sub_tiles(tm):
    sub = min(tm, FFN_SUB_ROWS)
    return [slice(a, a + sub) for a in range(0, tm, sub)]


def _silu_gate(cg, cv):
    return jax.nn.silu(cg) * cv


def _ffn_up_kernel(x_ref, w_ref, bg_ref, bv_ref, cwg_ref, cwv_ref, cbg_ref, cbv_ref,
                   h_ref, sg_ref, sv_ref, carry_g, carry_v, *, tiles_per_seq):
    i, j = pl.program_id(0), pl.program_id(1)
    tm, tn = h_ref.shape

    @pl.when(i % tiles_per_seq == 0)
    def _():
        carry_g[j, SUBLANES - 2:, :] = bg_ref[...]
        carry_v[j, SUBLANES - 2:, :] = bv_ref[...]

    def conv(u, prev, cw_ref, cb_ref):
        rows = lax.broadcasted_iota(jnp.int32, u.shape, 0)
        p1, p2 = prev[SUBLANES - 1:], prev[SUBLANES - 2:SUBLANES - 1]
        s1 = jnp.where(rows == 0, p1, pltpu.roll(u, 1, axis=0))
        s2 = jnp.where(rows == 0, p2, jnp.where(rows == 1, p1, pltpu.roll(u, 2, axis=0)))
        cw = cw_ref[...]
        return cb_ref[...] + cw[0:1] * s2 + cw[1:2] * s1 + cw[2:3] * u, u[u.shape[0] - SUBLANES:]

    prev_g, prev_v = carry_g[j], carry_v[j]
    for rs in _ffn_sub_tiles(tm):
        up = jnp.dot(x_ref[rs, :], w_ref[...], preferred_element_type=F32)
        cg, prev_g = conv(up[:, :tn], prev_g, cwg_ref, cbg_ref)
        cv, prev_v = conv(up[:, tn:], prev_v, cwv_ref, cbv_ref)
        h_ref[rs, :] = _silu_gate(cg, cv).astype(h_ref.dtype)
    carry_g[j], carry_v[j] = prev_g, prev_v
    sg_ref[...], sv_ref[...] = prev_g, prev_v


def _ffn_weight_kernel(g_ref, v_ref, o_ref):
    o_ref[:, :FFN_TILE] = g_ref[...].astype(o_ref.dtype)
    o_ref[:, FFN_TILE:] = v_ref[...].astype(o_ref.dtype)


def _interleave_ffn_weight(w):
    nl, k, f2 = w.shape
    nv = f2 // (2 * FFN_TILE)
    tk = _pick(k, (2048, 1024, 512, 256, 128))
    return pl.pallas_call(
        _ffn_weight_kernel,
        out_shape=jax.ShapeDtypeStruct((nl, k, f2), BF16),
        grid=(nl, k // tk, nv),
        in_specs=[pl.BlockSpec((None, tk, FFN_TILE), lambda a, i, j: (a, i, j)),
                  pl.BlockSpec((None, tk, FFN_TILE), lambda a, i, j: (a, i, j + nv))],
        out_specs=pl.BlockSpec((None, tk, 2 * FFN_TILE), lambda a, i, j: (a, i, j)),
        compiler_params=_cparams("parallel", "parallel", "parallel"),
        name="ffn_weight_interleave",
    )(w, w)


def _deinterleave_ffn_tiles(a):
    lead, f2 = a.shape[:-1], a.shape[-1]
    a = a.reshape(*lead, f2 // (2 * FFN_TILE), 2, FFN_TILE)
    return jnp.swapaxes(a, -3, -2).reshape(*lead, f2)


def _ffn_up_prompt(x, w_up_il, layer, buf, conv_w, conv_b, *, batch):
    bt, k = x.shape
    f2 = w_up_il.shape[2]
    f = f2 // 2
    t = bt // batch
    tm = _pick(t, (1024, 512, 256, 128))
    tps = t // tm
    nv = f // FFN_TILE
    half = lambda blk, fn: (pl.BlockSpec(blk, lambda i, j: fn(i, j, 0)), pl.BlockSpec(blk, lambda i, j: fn(i, j, nv)))
    w_spec = pl.BlockSpec((None, k, 2 * FFN_TILE), lambda i, j: (layer, 0, j))
    b_specs = half((None, CONV_WIDTH - 1, FFN_TILE), lambda i, j, o: (i // tps, 0, j + o))
    cw_specs = half((CONV_WIDTH, FFN_TILE), lambda i, j, o: (0, j + o))
    cb_specs = half((1, FFN_TILE), lambda i, j, o: (0, j + o))
    state_spec = pl.BlockSpec((None, SUBLANES, FFN_TILE), lambda i, j: (i, 0, j))
    h, tails_g, tails_v = pl.pallas_call(
        functools.partial(_ffn_up_kernel, tiles_per_seq=tps),
        out_shape=[jax.ShapeDtypeStruct((bt, f), BF16),
                   jax.ShapeDtypeStruct((bt // tm, SUBLANES, f), F32),
                   jax.ShapeDtypeStruct((bt // tm, SUBLANES, f), F32)],
        grid=(bt // tm, nv),
        in_specs=[pl.BlockSpec((tm, k), lambda i, j: (i, 0)), w_spec, *b_specs, *cw_specs, *cb_specs],
        out_specs=[pl.BlockSpec((tm, FFN_TILE), lambda i, j: (i, j)), state_spec, state_spec],
        scratch_shapes=[pltpu.VMEM((nv, SUBLANES, FFN_TILE), F32), pltpu.VMEM((nv, SUBLANES, FFN_TILE), F32)],
        compiler_params=_cparams("arbitrary", "arbitrary"),
        name="ffn_up_conv_p",
    )(x, w_up_il, buf, buf, conv_w, conv_w, conv_b.reshape(1, f2), conv_b.reshape(1, f2))
    return h, tails_g[tps - 1::tps], tails_v[tps - 1::tps]


def _conv_sample_kernel(x2g, x1g, x0g, x2v, x1v, x0v, wg_ref, wv_ref, cbg_ref, cbv_ref, o_ref):
    wg, wv = wg_ref[...], wv_ref[...]
    cg = cbg_ref[...] + wg[0:1] * x2g[...] + wg[1:2] * x1g[...] + wg[2:3] * x0g[...]
    cv = cbv_ref[...] + wv[0:1] * x2v[...] + wv[1:2] * x1v[...] + wv[2:3] * x0v[...]
    o_ref[...] = _silu_gate(cg, cv).astype(o_ref.dtype)


def _conv_gate_sample(hp, conv_w, conv_b):
    bd, t2, f2 = hp.shape
    t = t2 - (CONV_WIDTH - 1)
    f = f2 // 2
    nv = f // FFN_TILE
    shifted = [hp[:, s:s + t].reshape(bd * t, f2) for s in range(CONV_WIDTH)]
    xs = lambda off: pl.BlockSpec((bd * t, FFN_TILE), lambda j: (0, j + off))
    taps = lambda off: pl.BlockSpec((CONV_WIDTH, FFN_TILE), lambda j: (0, j + off))
    cb = lambda off: pl.BlockSpec((1, FFN_TILE), lambda j: (0, j + off))
    return pl.pallas_call(
        _conv_sample_kernel,
        out_shape=jax.ShapeDtypeStruct((bd * t, f), BF16),
        grid=(nv,),
        in_specs=[xs(0)] * 3 + [xs(nv)] * 3 + [taps(0), taps(nv), cb(0), cb(nv)],
        out_specs=pl.BlockSpec((bd * t, FFN_TILE), lambda j: (0, j)),
        compiler_params=_cparams("parallel"),
        name="conv_gate_sample",
    )(*shifted, *shifted, conv_w, conv_w, conv_b.reshape(1, f2), conv_b.reshape(1, f2))


def kernel(x_prompt, x_sample, cache_attn_k, cache_attn_v, cache_mem_k, cache_mem_v, state_ssm_re, state_ssm_im, state_ffn_conv, page_table, mem_prompt, norm_mix, norm_cross, norm_ffn, norm_final, ssm_w_in, ssm_a_re, ssm_a_im, ssm_log_dt, ssm_b_re, ssm_b_im, ssm_c_re, ssm_c_im, ssm_d, ssm_w_glu, attn_w_qkv, attn_lambda, attn_subln, attn_w_o, rel_bias, cross_w_q, cross_w_kv, cross_w_o, ffn_w_up, ffn_conv_w, ffn_conv_b, ffn_w_down):
    bp, seq, dm = x_prompt.shape
    bd, dec_t, _ = x_sample.shape
    depth = norm_mix.shape[0]
    page, nh, hv = cache_attn_k.shape[2:]
    n_pages = page_table.shape[1]
    n_mem, mem_heads, mem_hd = cache_mem_k.shape[2:]
    mw = mem_heads * mem_hd
    d_ff = ffn_w_down.shape[1]
    nb = rel_bias.shape[0]
    assert hv == LANES and mem_hd == LANES and d_ff % FFN_TILE == 0
    attn_scale = (hv // 2) ** -0.5

    xp = x_prompt.reshape(bp * seq, dm)
    xs = x_sample.reshape(bd * dec_t, dm)
    mem_b = mem_prompt.reshape(bp * n_mem, dm).astype(BF16)

    blk = min(256, seq)
    far = _far_distance(nb)
    assert blk + 1 >= far and seq % blk == 0
    kpos, qpos = np.arange(blk)[:, None], np.arange(blk)[None, :]
    diag = np.where(kpos > qpos, -1, _rel_bucket_np(qpos - kpos, nb))
    idx_prompt = np.concatenate([diag, _rel_bucket_np(blk + qpos - kpos, nb)], axis=0)
    past = n_pages * page
    assert page + 1 >= far
    tt =np.arange(SUBLANES)[:, None]
    kk = np.arange(page)[None, :]
    tab_far = np.full((SUBLANES, page), nb - 1)
    tab_last = _rel_bucket_np(past + tt - ((n_pages - 1) * page + kk), nb)
    tab_new = np.where((kk <= tt) & (kk < dec_t) & (tt < dec_t), _rel_bucket_np(tt - kk, nb), -1)
    idx_sample = np.concatenate([tab_far, tab_last, tab_new], axis=0).astype(np.int32)

    w_in, w_glu = ssm_w_in.astype(BF16), ssm_w_glu.astype(BF16)
    w_qkv, w_o = attn_w_qkv.astype(BF16), attn_w_o.astype(BF16)
    w_q, w_kv, w_co = cross_w_q.astype(BF16), cross_w_kv.astype(BF16), cross_w_o.astype(BF16)
    w_up, w_down = _interleave_ffn_weight(ffn_w_up), ffn_w_down.astype(BF16)

    outs = {k: [] for k in ("ssm_re_p", "ssm_im_p", "k_p", "v_p", "mk_p", "mv_p", "conv_p",
                            "ssm_re_s", "ssm_im_s", "k_s", "v_s", "conv_s")}
    for i in range(depth):
        j = i // 2
        hp = _rmsnorm(xp, norm_mix[i], BF16)
        hs = _rmsnorm(xs, norm_mix[i], BF16)
        if i % 2 == 0:
            prep = _ssm_prep(ssm_a_re[j], ssm_a_im[j], ssm_log_dt[j], ssm_b_re[j], ssm_b_im[j],
                             ssm_c_re[j], ssm_c_im[j])
            up_ = _mm(hp, w_in, j, name="ssm_in_p")
            us_ = _mm(hs, w_in, j, name="ssm_in_s")
            zp, hrp, hip = _ssm_mixer_prompt(up_, prep, ssm_d[j], bp)
            zs, hrs, his = _ssm_mixer_sample(us_, prep, ssm_d[j], state_ssm_re[j], state_ssm_im[j])
            xp = _glu_mm(zp, w_glu, j, xp)
            xs = _glu_mm(zs, w_glu, j, xs)
            outs["ssm_re_p"].append(hrp)
            outs["ssm_im_p"].append(hip)
            outs["ssm_re_s"].append(hrs)
            outs["ssm_im_s"].append(his)
        else:
            lam_init = 0.8 - 0.6 * math.exp(-0.3 * i)
            width = nh * hv
            qkv = lambda h, tag: (_mm(h, w_qkv, j, n=width, out_dtype=BF16, scale=attn_scale, name="q_" + tag),
                                  _mm(h, w_qkv, j, n=width, col_off=width, name="k_" + tag),
                                  _mm(h, w_qkv, j, n=width, col_off=2 * width, name="v_" + tag))
            qp, kp_, vp_ = qkv(hp, "p")
            qs, ks_, vs_ = qkv(hs, "s")
            bias_p = _bias_tables(rel_bias, idx_prompt).reshape(nh, 2, blk, blk)
            tabs_s = _bias_tables(rel_bias, idx_sample).reshape(nh, 3, SUBLANES, page)
            ap = _attn_prompt(qp, kp_, vp_, bias_p, attn_lambda[j], attn_subln[j],
                              batch=bp, lam_init=lam_init, blk=blk)
            as_ = _attn_sample(qs, ks_, vs_, cache_attn_k, cache_attn_v, j, page_table, tabs_s,
                               attn_lambda[j], attn_subln[j], lam_init=lam_init)
            xp = _mm(ap, w_o, j, res=xp, name="attn_o_p")
            xs = _mm(as_, w_o, j, res=xs, name="attn_o_s")
            outs["k_p"].append(kp_.reshape(bp, seq, nh, hv))
            outs["v_p"].append(vp_.reshape(bp, seq, nh, hv))
            outs["k_s"].append(ks_.reshape(bd, dec_t, nh, hv))
            outs["v_s"].append(vs_.reshape(bd, dec_t, nh, hv))

        hp = _rmsnorm(xp, norm_cross[i], BF16)
        hs = _rmsnorm(xs, norm_cross[i], BF16)
        mkp = _mm(mem_b, w_kv, i, n=mw, name="mem_k")
        mvp = _mm(mem_b, w_kv, i, n=mw, col_off=mw, name="mem_v")
        outs["mk_p"].append(mkp.reshape(bp, n_mem, mem_heads, mem_hd))
        outs["mv_p"].append(mvp.reshape(bp, n_mem, mem_heads, mem_hd))
        cq_p = _mm(hp, w_q, i, out_dtype=BF16, name="cross_q_p")
        cq_s = _mm(hs, w_q, i, out_dtype=BF16, name="cross_q_s")
        co_p = _cross_attn(cq_p, mkp, mvp, batch=bp, shared_q=False)
        co_all = _cross_attn(cq_s, cache_mem_k[i].reshape(bd * n_mem, mw), cache_mem_v[i].reshape(bd * n_mem, mw),
                             batch=bd, shared_q=True)
        co_all = co_all.reshape(bd, bd, dec_t, mw)
        co_s = jnp.stack([co_all[b, b] for b in range(bd)]).reshape(bd * dec_t, mw)
        xp = _mm(co_p, w_co, i, res=xp, name="cross_o_p")
        xs = _mm(co_s, w_co, i, res=xs, name="cross_o_s")

        hp = _rmsnorm(xp, norm_ffn[i], BF16)
        hs = _rmsnorm(xs, norm_ffn[i], BF16)
        buf0 = jnp.zeros((bp, CONV_WIDTH - 1, 2 * d_ff), F32)
        gp, tail_g, tail_v = _ffn_up_prompt(hp, w_up, i, buf0, ffn_conv_w[i], ffn_conv_b[i], batch=bp)
        up_s = _deinterleave_ffn_tiles(_mm(hs, w_up, i, name="ffn_up_s"))
        hp_s = jnp.concatenate([state_ffn_conv[i], up_s.reshape(bd, dec_t, 2 * d_ff)], axis=1)
        gs = _conv_gate_sample(hp_s, ffn_conv_w[i], ffn_conv_b[i])
        xp = _mm(gp, w_down, i, res=xp, name="ffn_down_p")
        xs = _mm(gs, w_down, i, res=xs, name="ffn_down_s")
        keep = CONV_WIDTH - 1
        assert seq >= keep
        outs["conv_p"].append(jnp.concatenate([tail_g[:, -keep:], tail_v[:, -keep:]], axis=-1))
        outs["conv_s"].append(hp_s[:, -keep:])

    y_prompt = _rmsnorm(xp, norm_final, F32).reshape(bp, seq, dm)
    y_sample = _rmsnorm(xs, norm_final, F32).reshape(bd, dec_t, dm)
    st = lambda k: jnp.stack(outs[k])
    return (y_prompt, y_sample, st("ssm_re_p"), st("ssm_im_p"), st("k_p"), st("v_p"),
            st("mk_p"), st("mv_p"), st("conv_p"), st("ssm_re_s"), st("ssm_im_s"),
            st("k_s"), st("v_s"), st("conv_s"))
```

```python
import functools
import math

import jax
import jax.numpy as jnp
import numpy as np
from jax import lax
from jax.experimental import pallas as pl
from jax.experimental.pallas import tpu as pltpu

F32 = jnp.float32
BF16 = jnp.bfloat16

SSM_GROUP = 16
MAX_DISTANCE = 128
RMS_EPS = 1e-6
SUBLN_EPS = 1e-5
CONV_WIDTH = 3

LANES = 128
SUBLANES = 8
VMEM_LIMIT_BYTES = 56 * 1024 * 1024

MASK_VALUE = -1e30
NT_DIMS = (((1,), (1,)), ((), ()))


def _cparams(*sem):
    return pltpu.CompilerParams(dimension_semantics=sem, vmem_limit_bytes=VMEM_LIMIT_BYTES)


def _pick(n, cands):
    for c in cands:
        if n % c == 0:
            return c
    return n


def _rmsnorm_kernel(x_ref, g_ref, o_ref):
    x = x_ref[...]
    ms = jnp.mean(x * x, axis=-1, keepdims=True)
    o_ref[...] = (x * lax.rsqrt(ms + RMS_EPS) * g_ref[...]).astype(o_ref.dtype)


def _rmsnorm(x, g, out_dtype):
    m, d = x.shape
    tm = _pick(m, (256,))
    return pl.pallas_call(
        _rmsnorm_kernel,
        out_shape=jax.ShapeDtypeStruct((m, d), out_dtype),
        grid=(m // tm,),
        in_specs=[pl.BlockSpec((tm, d), lambda i: (i, 0)),
                  pl.BlockSpec((1, d), lambda i: (0, 0))],
        out_specs=pl.BlockSpec((tm, d), lambda i: (i, 0)),
        compiler_params=_cparams("parallel"),
        name="rmsnorm",
    )(x, g.reshape(1, d))


def _mm_kernel(*refs, nk, scale, has_res):
    if has_res:
        x_ref, w_ref, r_ref, o_ref = refs[:4]
        rest = refs[4:]
    else:
        x_ref, w_ref, o_ref = refs[:3]
        r_ref = None
        rest = refs[3:]

    def finish(acc):
        if scale is not None:
            acc = acc * scale
        if r_ref is not None:
            acc = r_ref[...] + acc
        o_ref[...] = acc.astype(o_ref.dtype)

    part = jnp.dot(x_ref[...], w_ref[...], preferred_element_type=F32)
    if nk == 1:
        finish(part)
        return
    acc_ref, = rest
    k = pl.program_id(2)

    @pl.when(k == 0)
    def _():
        acc_ref[...] = part

    @pl.when(k > 0)
    def _():
        acc_ref[...] += part

    @pl.when(k == nk - 1)
    def _():
        finish(acc_ref[...])


MAX_K_BLOCK = 6144


def _mm_tiles(m, n, k):
    tm = _pick(m, (1024,))
    if m >= 1024:
        tn = _pick(n, (512, 256, 128))
    else:
        tn = _pick(n, (1024, 512, 256, 128))
    tk = k if k <= MAX_K_BLOCK else _pick(k, range(MAX_K_BLOCK - MAX_K_BLOCK % LANES, 0, -LANES))
    return tm, tn, tk


def _mm(x, w, layer, *, n=None, col_off=0, out_dtype=F32, res=None, scale=None, name="mm"):
    m, k = x.shape
    n = w.shape[2] if n is None else n
    tm, tn, tk = _mm_tiles(m, n, k)
    nk = k // tk
    assert col_off % tn == 0 and m % tm == 0 and n % tn == 0 and k % tk == 0
    cb = col_off // tn
    in_specs = [pl.BlockSpec((tm, tk), lambda i, j, kk: (i, kk)),
                pl.BlockSpec((None, tk, tn), lambda i, j, kk: (layer, kk, j + cb))]
    args = [x, w]
    if res is not None:
        in_specs.append(pl.BlockSpec((tm, tn), lambda i, j, kk: (i, j)))
        args.append(res)
    return pl.pallas_call(
        functools.partial(_mm_kernel, nk=nk, scale=scale, has_res=res is not None),
        out_shape=jax.ShapeDtypeStruct((m, n), out_dtype),
        grid=(m // tm, n // tn, nk),
        in_specs=in_specs,
        out_specs=pl.BlockSpec((tm, tn), lambda i, j, kk: (i, j)),
        scratch_shapes=[pltpu.VMEM((tm, tn), F32)] if nk > 1 else [],
        compiler_params=_cparams("parallel", "parallel", "arbitrary"),
        name=name,
    )(*args)


def _glu_kernel(x_ref, wa_ref, wb_ref, r_ref, o_ref):
    x = x_ref[...]
    a = jnp.dot(x, wa_ref[...], preferred_element_type=F32)
    b = jnp.dot(x, wb_ref[...], preferred_element_type=F32)
    o_ref[...] = r_ref[...] + a * jax.nn.sigmoid(b)


def _glu_mm(x, w, layer, res):
    m, k = x.shape
    d = w.shape[2] // 2
    tm = _pick(m, (1024,))
    tn = _pick(d, (512, 256, 128))
    nj = d // tn
    return pl.pallas_call(
        _glu_kernel,
        out_shape=jax.ShapeDtypeStruct((m, d), F32),
        grid=(m // tm, nj),
        in_specs=[pl.BlockSpec((tm, k), lambda i, j: (i, 0)),
                  pl.BlockSpec((None, k, tn), lambda i, j: (layer, 0, j)),
                  pl.BlockSpec((None, k, tn), lambda i, j: (layer, 0, j + nj)),
                  pl.BlockSpec((tm, tn), lambda i, j: (i, j))],
        out_specs=pl.BlockSpec((tm, tn), lambda i, j: (i, j)),
        compiler_params=_cparams("parallel", "parallel"),
        name="glu_mm",
    )(x, w, w, res)


def _ssm_discretise(ar, ai, log_dt):
    dt = jnp.exp(log_dt)
    mag = jnp.exp(ar * dt)
    return mag * jnp.cos(ai * dt), mag * jnp.sin(ai * dt)


def _ssm_prep_kernel(are_ref, aim_ref, ldt_ref, arc_ref, aic_ref, ldc_ref, btr_ref, bti_ref, ctr_ref, cti_ref,
                     lr_ref, li_ref, wb_ref, wc_ref):
    lr_ref[...], li_ref[...] = _ssm_discretise(are_ref[...], aim_ref[...], ldt_ref[...])
    ar, ai = arc_ref[...], aic_ref[...]
    lr, li = _ssm_discretise(ar, ai, ldc_ref[...])
    den = ar * ar + ai * ai
    nr = lr - 1.0
    kr = (nr * ar + li * ai) / den
    ki = (li * ar - nr * ai) / den
    btr, bti = btr_ref[...], bti_ref[...]
    rows = SUBLANES * SSM_GROUP
    b_re = (kr * btr - ki * bti).reshape(rows, LANES)
    b_im = (kr * bti + ki * btr).reshape(rows, LANES)
    c_re = ctr_ref[...].reshape(rows, LANES)
    c_im = -cti_ref[...].reshape(rows, LANES)
    row_group = lax.broadcasted_iota(jnp.int32, (rows, LANES), 0) // SSM_GROUP
    lane_half = lax.broadcasted_iota(jnp.int32, (rows, LANES), 1) // (LANES // 2)
    nblk = SUBLANES // 2
    for half, (bsrc, csrc) in enumerate(((b_re, c_re), (b_im, c_im))):
        for mblk in range(nblk):
            keep = row_group == 2 * mblk + lane_half
            sl = slice((half * nblk + mblk) * LANES, (half * nblk + mblk + 1) * LANES)
            wb_ref[:, sl] = jnp.where(keep, bsrc, 0.0).astype(wb_ref.dtype)
            wc_ref[:, sl] = jnp.where(keep, csrc, 0.0).astype(wc_ref.dtype)


def _ssm_prep(a_re, a_im, log_dt, b_re, b_im, c_re, c_im):
    g, p = a_re.shape
    assert 2 * p == LANES and g % SUBLANES == 0
    nt = g // SUBLANES
    width = SUBLANES * LANES
    dup = lambda a: jnp.concatenate([a, a], axis=-1)
    btr = dup(jnp.swapaxes(b_re, 1, 2))
    bti = dup(jnp.swapaxes(b_im, 1, 2))
    ldt = jnp.broadcast_to(log_dt[:, None], (g, LANES))
    on_rows = lambda a: jnp.broadcast_to(a[:, None, :], (g, SSM_GROUP, LANES))
    vec =pl.BlockSpec((SUBLANES, LANES), lambda j: (j, 0))
    cube = pl.BlockSpec((SUBLANES, SSM_GROUP, LANES), lambda j: (j, 0, 0))
    wide = pl.BlockSpec((None, SUBLANES * SSM_GROUP, width), lambda j: (j, 0, 0))
    return pl.pallas_call(
        _ssm_prep_kernel,
        out_shape=[jax.ShapeDtypeStruct((g, LANES), F32), jax.ShapeDtypeStruct((g, LANES), F32),
                   jax.ShapeDtypeStruct((nt, SUBLANES * SSM_GROUP, width), BF16),
                   jax.ShapeDtypeStruct((nt, SUBLANES * SSM_GROUP, width), BF16)],
        grid=(nt,),
        in_specs=[vec, vec, vec, cube, cube, cube, cube, cube, cube, cube],
        out_specs=[vec, vec, wide, wide],
        compiler_params=_cparams("parallel"),
        name="ssm_prep",
    )(dup(a_re), dup(a_im), ldt, on_rows(dup(a_re)), on_rows(dup(a_im)), on_rows(ldt),
      btr, bti, dup(c_re), dup(c_im))


SCAN_CHUNK = 128
SCAN_UNROLL = 4


def _sublane_transpose(xs):
    sub = lax.broadcasted_iota(jnp.int32, xs[0].shape, 1)
    for d in (4, 2, 1):
        out = list(xs)
        for i in range(SUBLANES):
            if i & d == 0:
                a, b = xs[i], xs[i | d]
                clear = (sub & d) == 0
                out[i] = jnp.where(clear, a, pltpu.roll(b, d, axis=1))
                out[i | d] = jnp.where(clear, pltpu.roll(a, SUBLANES - d, axis=1), b)
        xs = out
    return xs


def _rows_by_token(blocks):
    n = blocks[0].shape[0] // SUBLANES
    ys = _sublane_transpose([b.reshape(n, SUBLANES, LANES) for b in blocks])
    return jnp.stack(ys, axis=1).reshape(n * SUBLANES * SUBLANES, LANES)


def _rows_by_sequence(x):
    n = x.shape[0] // (SUBLANES * SUBLANES)
    x4 = x.reshape(n, SUBLANES, SUBLANES, LANES)
    xs = _sublane_transpose([x4[:, t] for t in range(SUBLANES)])
    return [b.reshape(n * SUBLANES, LANES) for b in xs]


def _ssm_scan_kernel(*refs, paired, steps):
    if paired:
        u_refs, refs = refs[:SUBLANES], refs[SUBLANES:]
        (wb0_ref, wb1_ref, wc0_ref, wc1_ref, lr_ref, li_ref, d_ref, h0_ref), refs = refs[:8], refs[8:]
        z_refs, (hf_ref, bu_ref, hs_ref) = refs[:SUBLANES], refs[SUBLANES:]
    else:
        (u_ref, wb0_ref, wc0_ref, lr_ref, li_ref, d_ref, h0_ref,
         z_ref, hf_ref, bu_ref, hs_ref) = refs
    half = hs_ref.shape[1] // 2

    @pl.when(pl.program_id(1) == 0)
    def _():
        hs_ref[...] = h0_ref[...]

    if paired:
        u = _rows_by_token([r[...] for r in u_refs])
        odd = lax.broadcasted_iota(jnp.int32, (u.shape[0], 1), 0) % SUBLANES >= SUBLANES // 2
        ub = jnp.concatenate([jnp.where(odd, 0.0, u), jnp.where(odd, u, 0.0)], axis=1).astype(BF16)
        wb = jnp.concatenate([wb0_ref[...], wb1_ref[...]], axis=0)
    else:
        u = u_ref[...]
        ub, wb = u.astype(BF16), wb0_ref[...]
    bu_ref[...] = jnp.dot(ub, wb, preferred_element_type=F32)

    lr, li = lr_ref[...], li_ref[...]

    def step(t, carry):
        hr, hi = carry
        row = pl.multiple_of(t * SUBLANES, SUBLANES)
        blk = bu_ref[pl.ds(row, SUBLANES), :]
        nr = hr * lr - hi * li + blk[:, :half]
        ni = hr * li + hi * lr + blk[:, half:]
        bu_ref[pl.ds(row, SUBLANES), :] = jnp.concatenate([nr, ni], axis=1)
        return nr, ni

    h0 = hs_ref[...]
    hr, hi = lax.fori_loop(0, steps, step, (h0[:, :half], h0[:, half:]), unroll=SCAN_UNROLL)
    hfin = jnp.concatenate([hr, hi], axis=1)
    hs_ref[...] = hfin
    hf_ref[...] = hfin

    hs = bu_ref[...].astype(BF16)
    if paired:
        wc = jnp.concatenate([wc0_ref[...], wc1_ref[...]], axis=0)
        y2 = lax.dot_general(hs, wc, NT_DIMS, preferred_element_type=F32)
        y = jnp.where(odd, y2[:, LANES:], y2[:, :LANES])
    else:
        y = lax.dot_general(hs, wc0_ref[...], NT_DIMS, preferred_element_type=F32)
    y = (y.reshape(steps, SUBLANES, LANES) + d_ref[...][None] * u.reshape(steps, SUBLANES, LANES))
    z = jax.nn.gelu(y).reshape(steps * SUBLANES, LANES)
    if paired:
        for z_ref, zs in zip(z_refs, _rows_by_sequence(z)):
            z_ref[...] = zs.astype(z_ref.dtype)
    else:
        z_ref[...] = z.astype(z_ref.dtype)


def _ssm_scan(u_rows, wb, wc, lr_slab, li_slab, d_slab, h0, *, paired):
    rows, c = u_rows.shape
    width = wb.shape[2]
    off = wb.shape[0] // 2
    t = rows // (SUBLANES // 2 if paired else SUBLANES)
    nj = off if paired else c // LANES
    lt = _pick(t, (SCAN_CHUNK,))
    w_lo = pl.BlockSpec((None, LANES, width), lambda j, s: (j, 0, 0))
    w_hi = pl.BlockSpec((None, LANES, width), lambda j, s: (j + off, 0, 0))
    slab = lambda n: pl.BlockSpec((None, SUBLANES, n), lambda j, s: (j, 0, 0))
    if paired:
        assert t % lt == 0 and lt % SUBLANES == 0
        seq_blocks = t // lt
        u_specs = [pl.BlockSpec((lt, LANES), lambda j, s, b=b, h=h: (b * seq_blocks + s, j + h * off))
                   for h in range(2) for b in range(SUBLANES // 2)]
        in_specs = [*u_specs, w_lo, w_hi, w_lo, w_hi]
        args = [*([u_rows] * SUBLANES), wb, wb, wc, wc]
        z_shapes = [jax.ShapeDtypeStruct((t, c // 2), BF16)] * SUBLANES
        z_specs = [pl.BlockSpec((lt, LANES), lambda j, s: (s, j))] * SUBLANES
    else:
        u_spec = pl.BlockSpec((lt * SUBLANES, LANES), lambda j, s: (s, j))
        in_specs = [u_spec, w_lo, w_lo]
        args = [u_rows, wb, wc]
        z_shapes = [jax.ShapeDtypeStruct((rows, c), BF16)]
        z_specs = [u_spec]
    in_specs += [slab(width // 2), slab(width // 2), slab(LANES), slab(width)]
    args += [lr_slab, li_slab, d_slab, h0]
    return pl.pallas_call(
        functools.partial(_ssm_scan_kernel, paired=paired, steps=lt),
        out_shape=[*z_shapes, jax.ShapeDtypeStruct((nj, SUBLANES, width), F32)],
        grid=(nj, t // lt),
        in_specs=in_specs,
        out_specs=[*z_specs, slab(width)],
        scratch_shapes=[pltpu.VMEM((lt * SUBLANES, width), F32), pltpu.VMEM((SUBLANES, width), F32)],
        compiler_params=_cparams("parallel", "arbitrary"),
        name="ssm_scan",
    )(*args)


def _ssm_tiles(v, nt):
    return v[:, : LANES // 2].reshape(nt, SUBLANES * (LANES // 2))


def _ssm_mixer_prompt(u, prep, d, batch):
    lr, li, wb, wc = prep
    nt = wb.shape[0]
    bt, dm = u.shape
    t = bt // batch
    assert 2 * batch == SUBLANES and nt % 2 == 0
    hw = SUBLANES * (LANES // 2)
    def pair_slab(tiles):
        n = tiles.shape[1]
        s = jnp.stack([tiles[: nt // 2], tiles[nt // 2:]], axis=1)
        return jnp.broadcast_to(s[:, :, None], (nt // 2, 2, batch, n)).reshape(nt // 2, SUBLANES, n)

    *zs, hfin = _ssm_scan(u, wb, wc, pair_slab(_ssm_tiles(lr, nt)), pair_slab(_ssm_tiles(li, nt)),
                          pair_slab(d.reshape(nt, LANES)), jnp.zeros((nt // 2, SUBLANES, 2 * hw), F32),
                          paired=True)
    z = jnp.stack([jnp.concatenate([zs[b], zs[batch + b]], axis=-1) for b in range(batch)]).reshape(bt, dm)

    def states(h):
        h = h.reshape(nt // 2, 2, batch, SUBLANES, LANES // 2).transpose(2, 1, 0, 3, 4)
        return h.reshape(batch, nt * SUBLANES, LANES // 2)

    return z, states(hfin[..., :hw]), states(hfin[..., hw:])


def _ssm_mixer_sample(u, prep, d, h0_re, h0_im):
    lr, li, wb, wc = prep
    nt = wb.shape[0]
    bd = h0_re.shape[0]
    bt, dm = u.shape
    t = bt // bd
    assert bd == SUBLANES
    hw = SUBLANES * (LANES // 2)
    u_rows = u.reshape(bd, t, dm).transpose(1, 0, 2).reshape(t * bd, dm)
    slab = lambda tiles: jnp.broadcast_to(tiles[:, None], (nt, SUBLANES, tiles.shape[1]))
    to_tiles = lambda h: h.reshape(bd, nt, hw).transpose(1, 0, 2)
    h0 = jnp.concatenate([to_tiles(h0_re), to_tiles(h0_im)], axis=-1)
    z_rows, hfin = _ssm_scan(u_rows, wb, wc, slab(_ssm_tiles(lr, nt)), slab(_ssm_tiles(li, nt)),
                             slab(d.reshape(nt, LANES)), h0, paired=False)
    z = z_rows.reshape(t, bd, dm).transpose(1, 0, 2).reshape(bt, dm)
    states = lambda h: h.transpose(1, 0, 2).reshape(bd, nt * SUBLANES, LANES // 2)
    return z, states(hfin[..., :hw]), states(hfin[..., hw:])


def _rel_bucket_np(dist, num_buckets):
    n = np.maximum(dist, 0)
    max_exact = num_buckets // 2
    nf = np.maximum(n, 1).astype(np.float32)
    large = max_exact + (np.log(nf / np.float32(max_exact)) / np.float32(math.log(MAX_DISTANCE / max_exact))
                         * np.float32(num_buckets - max_exact)).astype(np.int32)
    large = np.minimum(large, num_buckets - 1)
    return np.where(n < max_exact, n, large).astype(np.int32)


def _far_distance(num_buckets):
    b = _rel_bucket_np(np.arange(4 * MAX_DISTANCE), num_buckets)
    return int(np.max(np.nonzero(b != num_buckets - 1)[0])) + 1


def _bias_kernel(rb_ref, idx_ref, o_ref, *, num_buckets):
    h = pl.program_id(0)
    idx = idx_ref[...]
    acc = jnp.full(idx.shape, MASK_VALUE, F32)
    for b in range(num_buckets):
        acc = jnp.where(idx == b, rb_ref[b, h], acc)
    o_ref[...] = acc


def _bias_tables(rel_bias, idx):
    nb, nh = rel_bias.shape
    r, c = idx.shape
    return pl.pallas_call(
        functools.partial(_bias_kernel, num_buckets=nb),
        out_shape=jax.ShapeDtypeStruct((nh, r, c), F32),
        grid=(nh,),
        in_specs=[pl.BlockSpec(memory_space=pltpu.SMEM),
                  pl.BlockSpec((r, c), lambda h: (0, 0))],
        out_specs=pl.BlockSpec((None, r, c), lambda h: (h, 0, 0)),
        compiler_params=_cparams("parallel"),
        name="rel_bias_tables",
    )(rel_bias, jnp.asarray(idx))


def _diff_lambda(lam_ref, lam_init):
    lp = lam_ref[...]
    e1 = jnp.exp(jnp.sum(lp[0:1] * lp[1:2], keepdims=True))
    e2 = jnp.exp(jnp.sum(lp[2:3] * lp[3:4], keepdims=True))
    return e1 - e2 + lam_init


def _diff_merge(o1, o2, lam, subln, lam_init):
    d = o1 - lam * o2
    d = d * lax.rsqrt(jnp.mean(d * d, axis=-1, keepdims=True) + SUBLN_EPS) * subln
    return d * (1.0 - lam_init)


ATTN_HEADS_PER_STEP = 4


def _attn_prompt_kernel(lam_ref, subln_ref, q_ref, k_ref, v_ref, bias_ref, o_ref,
                        kb_ref, vt_ref, m_ref, l_ref, acc_ref, *, blk, lam_init):
    qi = pl.program_id(2)
    heads = kb_ref.shape[0]
    e = LANES

    @pl.when(qi == 0)
    def _():
        for h in range(heads):
            for j in range(kb_ref.shape[1]):
                rows, cols = slice(j * blk, (j + 1) * blk), slice(h * e, (h + 1) * e)
                kb_ref[h, j] = k_ref[rows, cols].astype(BF16)
                vt_ref[h, j] = v_ref[rows, cols].T.astype(BF16)

    feat = lax.broadcasted_iota(jnp.int32, (e, blk), 0)
    qts = []
    for h in range(heads):
        qt = q_ref[:, h * e:(h + 1) * e].astype(F32).T
        qts.append(jnp.concatenate([jnp.where(feat < e // 2, qt, 0.0), jnp.where(feat >= e // 2, qt, 0.0)],
                                   axis=1).astype(BF16))

    m_ref[...] = jnp.full(m_ref.shape, MASK_VALUE, F32)
    l_ref[...] = jnp.zeros(l_ref.shape, F32)
    acc_ref[...] = jnp.zeros(acc_ref.shape, F32)

    def attend(kbs, tile):
        sts, ps, corrs = [], [], []
        for h in range(heads):
            keys = jnp.concatenate([kb_ref[h, kb] for kb in kbs], axis=0)
            sts.append(jnp.dot(keys, qts[h], preferred_element_type=F32))
        for h in range(heads):
            if tile is None:
                bias = bias_ref[h, 1, 0:1, blk - 1:blk]
            else:
                bt = bias_ref[h, tile]
                bias = jnp.concatenate([bt, bt], axis=1)
            st = sts[h] + bias
            m_prev = m_ref[h]
            m_new = jnp.maximum(m_prev, jnp.max(st, axis=0, keepdims=True))
            corr = jnp.exp(m_prev - m_new)
            p = jnp.exp(st - m_new)
            l_ref[h] = l_ref[h] * corr + jnp.sum(p, axis=0, keepdims=True)
            m_ref[h] = m_new
            ps.append(p.astype(BF16))
            corrs.append(corr)
        for h in range(heads):
            vals = jnp.concatenate([vt_ref[h, kb] for kb in kbs], axis=1)
            acc_ref[h] = acc_ref[h] * corrs[h] + jnp.dot(vals, ps[h], preferred_element_type=F32)

    n_far = jnp.maximum(qi - 1, 0)

    def far_body(i, carry):
        attend([2 * i, 2 * i + 1], None)
        return carry

    lax.fori_loop(0, n_far // 2, far_body, 0)

    @pl.when(n_far % 2 == 1)
    def _():
        attend([n_far - 1], None)

    @pl.when(qi >= 1)
    def _():
        attend([qi - 1], 1)

    attend([qi], 0)

    lam = _diff_lambda(lam_ref, lam_init)
    for h in range(heads):
        ot = acc_ref[h] / l_ref[h]
        dt = ot[:, :blk] - lam * ot[:, blk:]
        dt = dt * lax.rsqrt(jnp.mean(dt * dt, axis=0, keepdims=True) + SUBLN_EPS)
        o_ref[:, h * e:(h + 1) * e] = (dt.T * subln_ref[...] * (1.0 - lam_init)).astype(o_ref.dtype)


def _attn_prompt(q, k, v, bias, lam_p, subln, *, batch, lam_init, blk):
    bs, he = q.shape
    s = bs // batch
    e = LANES
    nh = he // e
    nq = s // blk
    hps = ATTN_HEADS_PER_STEP if nh % ATTN_HEADS_PER_STEP == 0 else 1
    w = hps * e
    return pl.pallas_call(
        functools.partial(_attn_prompt_kernel, blk=blk, lam_init=lam_init),
        out_shape=jax.ShapeDtypeStruct((bs, he), BF16),
        grid=(batch, nh // hps, nq),
        in_specs=[pl.BlockSpec(lam_p.shape, lambda b, h, i: (0, 0)),
                  pl.BlockSpec((1, e), lambda b, h, i: (0, 0)),
                  pl.BlockSpec((blk, w), lambda b, h, i: (b * nq + i, h)),
                  pl.BlockSpec((s, w), lambda b, h, i: (b, h)),
                  pl.BlockSpec((s, w), lambda b, h, i: (b, h)),
                  pl.BlockSpec((hps, 2, blk, blk), lambda b, h, i: (h, 0, 0, 0))],
        out_specs=pl.BlockSpec((blk, w), lambda b, h, i: (b * nq + i, h)),
        scratch_shapes=[pltpu.VMEM((hps, nq, blk, e), BF16), pltpu.VMEM((hps, nq, e, blk), BF16),
                        pltpu.VMEM((hps, 1, 2 * blk), F32), pltpu.VMEM((hps, 1, 2 * blk), F32),
                        pltpu.VMEM((hps, e, 2 * blk), F32)],
        compiler_params=_cparams("parallel", "parallel", "arbitrary"),
        name="diff_attn_prompt",
    )(lam_p, subln.reshape(1, e), q, k, v, bias)


HEAD_GROUP = SUBLANES
DECODE_PAGES_PER_STEP = 4


def _attn_sample_kernel(pt_ref, lam_ref, subln_ref, q_ref, *refs, n_steps, pps, lam_init):
    ck_refs, cv_refs = refs[:pps], refs[pps:2 * pps]
    kn_ref, vn_ref, bias_far_ref, bias_ref, biasn_ref, o_ref, m_ref, l_ref, acc_ref = refs[2 * pps:]
    p = pl.program_id(1)
    ng = q_ref.shape[0]

    @pl.when(p == 0)
    def _():
        m_ref[...] = jnp.full(m_ref.shape, MASK_VALUE, F32)
        l_ref[...] = jnp.zeros(l_ref.shape, F32)
        acc_ref[...] = jnp.zeros(acc_ref.shape, F32)

    def attend(k_ref, v_ref, b_ref):
        keys = k_ref.shape[0]
        group = lambda ref, g: ref[:, g * HEAD_GROUP:(g + 1) * HEAD_GROUP, :].reshape(
            keys * HEAD_GROUP, LANES).astype(BF16)
        scores = [lax.dot_general(q_ref[g], group(k_ref, g), NT_DIMS, preferred_element_type=F32)
                  for g in range(ng)]
        probs, corrs = [], []
        for g in range(ng):
            s = scores[g] + b_ref[g]
            m_prev = m_ref[g]
            m_new = jnp.maximum(m_prev, jnp.max(s, axis=-1, keepdims=True))
            corr = jnp.exp(m_prev - m_new)
            pe = jnp.exp(s - m_new)
            l_ref[g] = l_ref[g] * corr + jnp.sum(pe, axis=-1, keepdims=True)
            m_ref[g] = m_new
            probs.append(pe.astype(BF16))
            corrs.append(corr)
        for g in range(ng):
            acc_ref[g] = acc_ref[g] * corrs[g] + jnp.dot(probs[g], group(v_ref, g), preferred_element_type=F32)

    @pl.when(p < n_steps)
    def _():
        for r in range(pps):
            attend(ck_refs[r], cv_refs[r], bias_ref if r == pps - 1 else bias_far_ref)

    @pl.when(p == n_steps)
    def _():
        attend(kn_ref, vn_ref, biasn_ref)
        o = acc_ref[...] / l_ref[...]
        half = o.shape[1] // 2
        lam = _diff_lambda(lam_ref, lam_init)
        o_ref[...] = _diff_merge(o[:, :half], o[:, half:], lam, subln_ref[...], lam_init).astype(o_ref.dtype)


def _attn_sample(q, k_new, v_new, cache_k, cache_v, layer, page_table, tabs, lam_p, subln, *, lam_init):
    bd, n_pages = page_table.shape
    page, nh, e = cache_k.shape[2:]
    t = q.shape[0] // bd
    assert e == LANES and nh % HEAD_GROUP == 0 and t <= SUBLANES
    ng = nh // HEAD_GROUP
    hh = e // 2
    rows = 2 * HEAD_GROUP * t

    q5 = q.reshape(bd, t, ng, HEAD_GROUP, e).transpose(0, 2, 3, 1, 4)
    lane = jnp.arange(e) < hh
    qm = jnp.stack([jnp.where(lane, q5, 0), jnp.where(lane, 0, q5)], axis=2)
    qm = qm.reshape(bd, ng, rows, e)

    same = np.arange(HEAD_GROUP)[:, None] == np.arange(HEAD_GROUP)[None, :]

    def expand(tab, keys):
        tb = tab[:, :t, :keys].reshape(ng, HEAD_GROUP, t, keys)
        full = jnp.where(same[None, :, None, None, :], tb[..., None], MASK_VALUE)
        full = jnp.broadcast_to(full[:, None], (ng, 2, HEAD_GROUP, t, keys, HEAD_GROUP))
        return full.reshape(ng, rows, keys * HEAD_GROUP)

    bias_pages = jnp.stack([expand(tabs[:, 0], page), expand(tabs[:, 1], page)])
    bias_new = expand(tabs[:, 2], SUBLANES)

    pad = lambda a: jnp.pad(a.reshape(bd, t, nh, e), ((0, 0), (0, SUBLANES - t), (0, 0), (0, 0)))
    pps = _pick(n_pages, (DECODE_PAGES_PER_STEP, 2, 1))
    n_steps = n_pages // pps
    page_maps = [lambda b, p, pt, r=r: (layer, pt[b, jnp.minimum(p, n_steps - 1) * pps + r], 0, 0, 0)
                 for r in range(pps)]
    page_specs = [pl.BlockSpec((None, None, page, nh, e), pm) for pm in page_maps]
    const2 = lambda b, p, pt: (0, 0)
    per_b = lambda b, p, pt: (b, 0, 0, 0)
    bias_spec = lambda sel: pl.BlockSpec((None, ng, rows, page * HEAD_GROUP), lambda b, p, pt: (sel(p), 0, 0, 0))
    out = pl.pallas_call(
        functools.partial(_attn_sample_kernel, n_steps=n_steps, pps=pps, lam_init=lam_init),
        out_shape=jax.ShapeDtypeStruct((bd, ng, rows // 2, e), BF16),
        grid_spec=pltpu.PrefetchScalarGridSpec(
            num_scalar_prefetch=1,
            grid=(bd, n_steps + 1),
            in_specs=[pl.BlockSpec(lam_p.shape, const2),
                      pl.BlockSpec((1, e), const2),
                      pl.BlockSpec((None, ng, rows, e), per_b),
                      *page_specs, *page_specs,
                      pl.BlockSpec((None, SUBLANES, nh, e), per_b),
                      pl.BlockSpec((None, SUBLANES, nh, e), per_b),
                      bias_spec(lambda p: 0),
                      bias_spec(lambda p: jnp.where(p >= n_steps - 1, 1, 0)),
                      pl.BlockSpec((ng, rows, SUBLANES * HEAD_GROUP), lambda b, p, pt: (0, 0, 0))],
            out_specs=pl.BlockSpec((None, ng, rows // 2, e), per_b),
            scratch_shapes=[pltpu.VMEM((ng, rows, 1), F32), pltpu.VMEM((ng, rows, 1), F32),
                            pltpu.VMEM((ng, rows, e), F32)]),
        compiler_params=_cparams("parallel", "arbitrary"),
        name="diff_attn_sample",
    )(page_table, lam_p, subln.reshape(1, e), qm, *([cache_k] * pps), *([cache_v] * pps),
      pad(k_new), pad(v_new), bias_pages, bias_pages, bias_new)
    return out.reshape(bd, ng, HEAD_GROUP, t, e).transpose(0, 3, 1, 2, 4).reshape(bd * t, nh * e)


def _cross_kernel(q_ref, k_ref, v_ref, o_ref, *, heads, scale):
    q = q_ref[...]
    k = k_ref[...].astype(BF16)
    v = v_ref[...].astype(BF16)
    outs = []
    for h in range(heads):
        sl = slice(h * LANES, (h + 1) * LANES)
        s = lax.dot_general(q[:, sl], k[:, sl], NT_DIMS, preferred_element_type=F32) * scale
        e = jnp.exp(s - jnp.max(s, axis=-1, keepdims=True))
        o = jnp.dot(e.astype(BF16), v[:, sl], preferred_element_type=F32)
        outs.append(o / jnp.sum(e, axis=-1, keepdims=True))
    o_ref[...] = jnp.concatenate(outs, axis=1).astype(o_ref.dtype)


def _cross_attn(q, mk, mv, *, batch, shared_q):
    rows, w = q.shape
    nm = mk.shape[0] // batch
    heads = w // LANES
    if shared_q:
        tq, nq = rows, 1
        q_map = lambda b, i: (0, 0)
    else:
        per = rows // batch
        tq = _pick(per, (512, 256, 128))
        nq = per // tq
        q_map = lambda b, i: (b * nq + i, 0)
    return pl.pallas_call(
        functools.partial(_cross_kernel, heads=heads, scale=LANES ** -0.5),
        out_shape=jax.ShapeDtypeStruct((batch * nq * tq, w), BF16),
        grid=(batch, nq),
        in_specs=[pl.BlockSpec((tq, w), q_map),
                  pl.BlockSpec((nm, w), lambda b, i: (b, 0)),
                  pl.BlockSpec((nm, w), lambda b, i: (b, 0))],
        out_specs=pl.BlockSpec((tq, w), lambda b, i: (b * nq + i, 0)),
        compiler_params=_cparams("parallel", "parallel"),
        name="cross_attn",
    )(q, mk, mv)


FFN_TILE = 256
FFN_SUB_ROWS = 256


def _ffn_sub_tiles(tm):
    sub = min(tm, FFN_SUB_ROWS)
    return [slice(a, a + sub) for a in range(0, tm, sub)]


def _silu_gate(cg, cv):
    return jax.nn.silu(cg) * cv


def _ffn_up_kernel(x_ref, w_ref, bg_ref, bv_ref, cwg_ref, cwv_ref, cbg_ref, cbv_ref,
                   h_ref, sg_ref, sv_ref, carry_g, carry_v, *, tiles_per_seq):
    i, j = pl.program_id(0), pl.program_id(1)
    tm, tn = h_ref.shape

    @pl.when(i % tiles_per_seq == 0)
    def _():
        carry_g[j, SUBLANES - 2:, :] = bg_ref[...]
        carry_v[j, SUBLANES - 2:, :] = bv_ref[...]

    def conv(u, prev, cw_ref, cb_ref):
        rows = lax.broadcasted_iota(jnp.int32, u.shape, 0)
        p1, p2 = prev[SUBLANES - 1:], prev[SUBLANES - 2:SUBLANES - 1]
        s1 = jnp.where(rows == 0, p1, pltpu.roll(u, 1, axis=0))
        s2 = jnp.where(rows == 0, p2, jnp.where(rows == 1, p1, pltpu.roll(u, 2, axis=0)))
        cw = cw_ref[...]
        return cb_ref[...] + cw[0:1] * s2 + cw[1:2] * s1 + cw[2:3] * u, u[u.shape[0] - SUBLANES:]

    prev_g, prev_v = carry_g[j], carry_v[j]
    for rs in _ffn_sub_tiles(tm):
        up = jnp.dot(x_ref[rs, :], w_ref[...], preferred_element_type=F32)
        cg, prev_g = conv(up[:, :tn], prev_g, cwg_ref, cbg_ref)
        cv, prev_v = conv(up[:, tn:], prev_v, cwv_ref, cbv_ref)
        h_ref[rs, :] = _silu_gate(cg, cv).astype(h_ref.dtype)
    carry_g[j], carry_v[j] = prev_g, prev_v
    sg_ref[...], sv_ref[...] = prev_g, prev_v


def _ffn_weight_kernel(g_ref, v_ref, o_ref):
    o_ref[:, :FFN_TILE] = g_ref[...].astype(o_ref.dtype)
    o_ref[:, FFN_TILE:] = v_ref[...].astype(o_ref.dtype)


def _interleave_ffn_weight(w):
    nl, k, f2 = w.shape
    nv = f2 // (2 * FFN_TILE)
    tk = _pick(k, (2048, 1024, 512, 256, 128))
    return pl.pallas_call(
        _ffn_weight_kernel,
        out_shape=jax.ShapeDtypeStruct((nl, k, f2), BF16),
        grid=(nl, k // tk, nv),
        in_specs=[pl.BlockSpec((None, tk, FFN_TILE), lambda a, i, j: (a, i, j)),
                  pl.BlockSpec((None, tk, FFN_TILE), lambda a, i, j: (a, i, j + nv))],
        out_specs=pl.BlockSpec((None, tk, 2 * FFN_TILE), lambda a, i, j: (a, i, j)),
        compiler_params=_cparams("parallel", "parallel", "parallel"),
        name="ffn_weight_interleave",
    )(w, w)


def _deinterleave_ffn_tiles(a):
    lead, f2 = a.shape[:-1], a.shape[-1]
    a = a.reshape(*lead, f2 // (2 * FFN_TILE), 2, FFN_TILE)
    return jnp.swapaxes(a, -3, -2).reshape(*lead, f2)


def _ffn_up_prompt(x, w_up_il, layer, buf, conv_w, conv_b, *, batch):
    bt, k = x.shape
    f2 = w_up_il.shape[2]
    f = f2 // 2
    t = bt // batch
    tm = _pick(t, (1024, 512, 256, 128))
    tps = t // tm
    nv = f // FFN_TILE
    half = lambda blk, fn: (pl.BlockSpec(blk, lambda i, j: fn(i, j, 0)), pl.BlockSpec(blk, lambda i, j: fn(i, j, nv)))
    w_spec = pl.BlockSpec((None, k, 2 * FFN_TILE), lambda i, j: (layer, 0, j))
    b_specs = half((None, CONV_WIDTH - 1, FFN_TILE), lambda i, j, o: (i // tps, 0, j + o))
    cw_specs = half((CONV_WIDTH, FFN_TILE), lambda i, j, o: (0, j + o))
    cb_specs = half((1, FFN_TILE), lambda i, j, o: (0, j + o))
    state_spec = pl.BlockSpec((None, SUBLANES, FFN_TILE), lambda i, j: (i, 0, j))
    h, tails_g, tails_v = pl.pallas_call(
        functools.partial(_ffn_up_kernel, tiles_per_seq=tps),
        out_shape=[jax.ShapeDtypeStruct((bt, f), BF16),
                   jax.ShapeDtypeStruct((bt // tm, SUBLANES, f), F32),
                   jax.ShapeDtypeStruct((bt // tm, SUBLANES, f), F32)],
        grid=(bt // tm, nv),
        in_specs=[pl.BlockSpec((tm, k), lambda i, j: (i, 0)), w_spec, *b_specs, *cw_specs, *cb_specs],
        out_specs=[pl.BlockSpec((tm, FFN_TILE), lambda i, j: (i, j)), state_spec, state_spec],
        scratch_shapes=[pltpu.VMEM((nv, SUBLANES, FFN_TILE), F32), pltpu.VMEM((nv, SUBLANES, FFN_TILE), F32)],
        compiler_params=_cparams("arbitrary", "arbitrary"),
        name="ffn_up_conv_p",
    )(x, w_up_il, buf, buf, conv_w, conv_w, conv_b.reshape(1, f2), conv_b.reshape(1, f2))
    return h, tails_g[tps - 1::tps], tails_v[tps - 1::tps]


def _conv_sample_kernel(x2g, x1g, x0g, x2v, x1v, x0v, wg_ref, wv_ref, cbg_ref, cbv_ref, o_ref):
    wg, wv = wg_ref[...], wv_ref[...]
    cg = cbg_ref[...] + wg[0:1] * x2g[...] + wg[1:2] * x1g[...] + wg[2:3] * x0g[...]
    cv = cbv_ref[...] + wv[0:1] * x2v[...] + wv[1:2] * x1v[...] + wv[2:3] * x0v[...]
    o_ref[...] = _silu_gate(cg, cv).astype(o_ref.dtype)


def _conv_gate_sample(hp, conv_w, conv_b):
    bd, t2, f2 = hp.shape
    t = t2 - (CONV_WIDTH - 1)
    f = f2 // 2
    nv = f // FFN_TILE
    shifted = [hp[:, s:s + t].reshape(bd * t, f2) for s in range(CONV_WIDTH)]
    xs = lambda off: pl.BlockSpec((bd * t, FFN_TILE), lambda j: (0, j + off))
    taps = lambda off: pl.BlockSpec((CONV_WIDTH, FFN_TILE), lambda j: (0, j + off))
    cb = lambda off: pl.BlockSpec((1, FFN_TILE), lambda j: (0, j + off))
    return pl.pallas_call(
        _conv_sample_kernel,
        out_shape=jax.ShapeDtypeStruct((bd * t, f), BF16),
        grid=(nv,),
        in_specs=[xs(0)] * 3 + [xs(nv)] * 3 + [taps(0), taps(nv), cb(0), cb(nv)],
        out_specs=pl.BlockSpec((bd * t, FFN_TILE), lambda j: (0, j)),
        compiler_params=_cparams("parallel"),
        name="conv_gate_sample",
    )(*shifted, *shifted, conv_w, conv_w, conv_b.reshape(1, f2), conv_b.reshape(1, f2))


def kernel(x_prompt, x_sample, cache_attn_k, cache_attn_v, cache_mem_k, cache_mem_v, state_ssm_re, state_ssm_im, state_ffn_conv, page_table, mem_prompt, norm_mix, norm_cross, norm_ffn, norm_final, ssm_w_in, ssm_a_re, ssm_a_im, ssm_log_dt, ssm_b_re, ssm_b_im, ssm_c_re, ssm_c_im, ssm_d, ssm_w_glu, attn_w_qkv, attn_lambda, attn_subln, attn_w_o, rel_bias, cross_w_q, cross_w_kv, cross_w_o, ffn_w_up, ffn_conv_w, ffn_conv_b, ffn_w_down):
    bp, seq, dm = x_prompt.shape
    bd, dec_t, _ = x_sample.shape
    depth = norm_mix.shape[0]
    page, nh, hv = cache_attn_k.shape[2:]
    n_pages = page_table.shape[1]
    n_mem, mem_heads, mem_hd = cache_mem_k.shape[2:]
    mw = mem_heads * mem_hd
    d_ff = ffn_w_down.shape[1]
    nb = rel_bias.shape[0]
    assert hv == LANES and mem_hd == LANES and d_ff % FFN_TILE == 0
    attn_scale = (hv // 2) ** -0.5

    xp = x_prompt.reshape(bp * seq, dm)
    xs = x_sample.reshape(bd * dec_t, dm)
    mem_b = mem_prompt.reshape(bp * n_mem, dm).astype(BF16)

    blk = min(256, seq)
    far = _far_distance(nb)
    assert blk + 1 >= far and seq % blk == 0
    kpos, qpos = np.arange(blk)[:, None], np.arange(blk)[None, :]
    diag = np.where(kpos > qpos, -1, _rel_bucket_np(qpos - kpos, nb))
    idx_prompt = np.concatenate([diag, _rel_bucket_np(blk + qpos - kpos, nb)], axis=0)
    past = n_pages * page
    assert page + 1 >= far
    tt =np.arange(SUBLANES)[:, None]
    kk = np.arange(page)[None, :]
    tab_far = np.full((SUBLANES, page), nb - 1)
    tab_last = _rel_bucket_np(past + tt - ((n_pages - 1) * page + kk), nb)
    tab_new = np.where((kk <= tt) & (kk < dec_t) & (tt < dec_t), _rel_bucket_np(tt - kk, nb), -1)
    idx_sample = np.concatenate([tab_far, tab_last, tab_new], axis=0).astype(np.int32)

    w_in, w_glu = ssm_w_in.astype(BF16), ssm_w_glu.astype(BF16)
    w_qkv, w_o = attn_w_qkv.astype(BF16), attn_w_o.astype(BF16)
    w_q, w_kv, w_co = cross_w_q.astype(BF16), cross_w_kv.astype(BF16), cross_w_o.astype(BF16)
    w_up, w_down = _interleave_ffn_weight(ffn_w_up), ffn_w_down.astype(BF16)

    outs = {k: [] for k in ("ssm_re_p", "ssm_im_p", "k_p", "v_p", "mk_p", "mv_p", "conv_p",
                            "ssm_re_s", "ssm_im_s", "k_s", "v_s", "conv_s")}
    for i in range(depth):
        j = i // 2
        hp = _rmsnorm(xp, norm_mix[i], BF16)
        hs = _rmsnorm(xs, norm_mix[i], BF16)
        if i % 2 == 0:
            prep = _ssm_prep(ssm_a_re[j], ssm_a_im[j], ssm_log_dt[j], ssm_b_re[j], ssm_b_im[j],
                             ssm_c_re[j], ssm_c_im[j])
            up_ = _mm(hp, w_in, j, name="ssm_in_p")
            us_ = _mm(hs, w_in, j, name="ssm_in_s")
            zp, hrp, hip = _ssm_mixer_prompt(up_, prep, ssm_d[j], bp)
            zs, hrs, his = _ssm_mixer_sample(us_, prep, ssm_d[j], state_ssm_re[j], state_ssm_im[j])
            xp = _glu_mm(zp, w_glu, j, xp)
            xs = _glu_mm(zs, w_glu, j, xs)
            outs["ssm_re_p"].append(hrp)
            outs["ssm_im_p"].append(hip)
            outs["ssm_re_s"].append(hrs)
            outs["ssm_im_s"].append(his)
        else:
            lam_init = 0.8 - 0.6 * math.exp(-0.3 * i)
            width = nh * hv
            qkv = lambda h, tag: (_mm(h, w_qkv, j, n=width, out_dtype=BF16, scale=attn_scale, name="q_" + tag),
                                  _mm(h, w_qkv, j, n=width, col_off=width, name="k_" + tag),
                                  _mm(h, w_qkv, j, n=width, col_off=2 * width, name="v_" + tag))
            qp, kp_, vp_ = qkv(hp, "p")
            qs, ks_, vs_ = qkv(hs, "s")
            bias_p = _bias_tables(rel_bias, idx_prompt).reshape(nh, 2, blk, blk)
            tabs_s = _bias_tables(rel_bias, idx_sample).reshape(nh, 3, SUBLANES, page)
            ap = _attn_prompt(qp, kp_, vp_, bias_p, attn_lambda[j], attn_subln[j],
                              batch=bp, lam_init=lam_init, blk=blk)
            as_ = _attn_sample(qs, ks_, vs_, cache_attn_k, cache_attn_v, j, page_table, tabs_s,
                               attn_lambda[j], attn_subln[j], lam_init=lam_init)
            xp = _mm(ap, w_o, j, res=xp, name="attn_o_p")
            xs = _mm(as_, w_o, j, res=xs, name="attn_o_s")
            outs["k_p"].append(kp_.reshape(bp, seq, nh, hv))
            outs["v_p"].append(vp_.reshape(bp, seq, nh, hv))
            outs["k_s"].append(ks_.reshape(bd, dec_t, nh, hv))
            outs["v_s"].append(vs_.reshape(bd, dec_t, nh, hv))

        hp = _rmsnorm(xp, norm_cross[i], BF16)
        hs = _rmsnorm(xs, norm_cross[i], BF16)
        mkp = _mm(mem_b, w_kv, i, n=mw, name="mem_k")
        mvp = _mm(mem_b, w_kv, i, n=mw, col_off=mw, name="mem_v")
        outs["mk_p"].append(mkp.reshape(bp, n_mem, mem_heads, mem_hd))
        outs["mv_p"].append(mvp.reshape(bp, n_mem, mem_heads, mem_hd))
        cq_p = _mm(hp, w_q, i, out_dtype=BF16, name="cross_q_p")
        cq_s = _mm(hs, w_q, i, out_dtype=BF16, name="cross_q_s")
        co_p = _cross_attn(cq_p, mkp, mvp, batch=bp, shared_q=False)
        co_all = _cross_attn(cq_s, cache_mem_k[i].reshape(bd * n_mem, mw), cache_mem_v[i].reshape(bd * n_mem, mw),
                             batch=bd, shared_q=True)
        co_all = co_all.reshape(bd, bd, dec_t, mw)
        co_s = jnp.stack([co_all[b, b] for b in range(bd)]).reshape(bd * dec_t, mw)
        xp = _mm(co_p, w_co, i, res=xp, name="cross_o_p")
        xs = _mm(co_s, w_co, i, res=xs, name="cross_o_s")

        hp = _rmsnorm(xp, norm_ffn[i], BF16)
        hs = _rmsnorm(xs, norm_ffn[i], BF16)
        buf0 = jnp.zeros((bp, CONV_WIDTH - 1, 2 * d_ff), F32)
        gp, tail_g, tail_v = _ffn_up_prompt(hp, w_up, i, buf0, ffn_conv_w[i], ffn_conv_b[i], batch=bp)
        up_s = _deinterleave_ffn_tiles(_mm(hs, w_up, i, name="ffn_up_s"))
        hp_s = jnp.concatenate([state_ffn_conv[i], up_s.reshape(bd, dec_t, 2 * d_ff)], axis=1)
        gs = _conv_gate_sample(hp_s, ffn_conv_w[i], ffn_conv_b[i])
        xp = _mm(gp, w_down, i, res=xp, name="ffn_down_p")
        xs = _mm(gs, w_down, i, res=xs, name="ffn_down_s")
        keep = CONV_WIDTH - 1
        assert seq >= keep
        outs["conv_p"].append(jnp.concatenate([tail_g[:, -keep:], tail_v[:, -keep:]], axis=-1))
        outs["conv_s"].append(hp_s[:, -keep:])

    y_prompt = _rmsnorm(xp, norm_final, F32).reshape(bp, seq, dm)
    y_sample = _rmsnorm(xs, norm_final, F32).reshape(bd, dec_t, dm)
    st = lambda k: jnp.stack(outs[k])
    return (y_prompt, y_sample, st("ssm_re_p"), st("ssm_im_p"), st("k_p"), st("v_p"),
            st("mk_p"), st("mv_p"), st("conv_p"), st("ssm_re_s"), st("ssm_im_s"),
            st("k_s"), st("v_s"), st("conv_s"))
```

```python
import functools
import math

import jax
import jax.numpy as jnp
import numpy as np
from jax import lax
from jax.experimental import pallas as pl
from jax.experimental.pallas import tpu as pltpu

F32 = jnp.float32
BF16 = jnp.bfloat16

SSM_GROUP = 16
MAX_DISTANCE = 128
RMS_EPS = 1e-6
SUBLN_EPS = 1e-5
CONV_WIDTH = 3

LANES = 128
SUBLANES = 8
VMEM_LIMIT_BYTES = 56 * 1024 * 1024

MASK_VALUE = -1e30
NT_DIMS = (((1,), (1,)), ((), ()))


def _cparams(*sem):
    return pltpu.CompilerParams(dimension_semantics=sem, vmem_limit_bytes=VMEM_LIMIT_BYTES)


def _pick(n, cands):
    for c in cands:
        if n % c == 0:
            return c
    return n


def _rmsnorm_kernel(x_ref, g_ref, o_ref):
    x = x_ref[...]
    ms = jnp.mean(x * x, axis=-1, keepdims=True)
    o_ref[...] = (x * lax.rsqrt(ms + RMS_EPS) * g_ref[...]).astype(o_ref.dtype)


def _rmsnorm(x, g, out_dtype):
    m, d = x.shape
    tm = _pick(m, (512, 256))
    return pl.pallas_call(
        _rmsnorm_kernel,
        out_shape=jax.ShapeDtypeStruct((m, d), out_dtype),
        grid=(m // tm,),
        in_specs=[pl.BlockSpec((tm, d), lambda i: (i, 0)),
                  pl.BlockSpec((1, d), lambda i: (0, 0))],
        out_specs=pl.BlockSpec((tm, d), lambda i: (i, 0)),
        compiler_params=_cparams("parallel"),
        name="rmsnorm",
    )(x, g.reshape(1, d))


def _mm_kernel(*refs, nk, scale, has_res):
    if has_res:
        x_ref, w_ref, r_ref, o_ref = refs[:4]
        rest = refs[4:]
    else:
        x_ref, w_ref, o_ref = refs[:3]
        r_ref = None
        rest = refs[3:]

    def finish(acc):
        if scale is not None:
            acc = acc * scale
        if r_ref is not None:
            acc = r_ref[...] + acc
        o_ref[...] = acc.astype(o_ref.dtype)

    part = jnp.dot(x_ref[...], w_ref[...], preferred_element_type=F32)
    if nk == 1:
        finish(part)
        return
    acc_ref, = rest
    k = pl.program_id(2)

    @pl.when(k == 0)
    def _():
        acc_ref[...] = part

    @pl.when(k > 0)
    def _():
        acc_ref[...] += part

    @pl.when(k == nk - 1)
    def _():
        finish(acc_ref[...])


MAX_K_BLOCK = 6144


def _mm_tiles(m, n, k, wide):
    tm = _pick(m, (1024,))
    if m >= 1024:
        tn = _pick(n, (1024, 512, 256, 128) if wide and k <= 4096 else (512, 256, 128))
    else:
        tn = _pick(n, (1024, 512, 256, 128))
    tk = k if k <= MAX_K_BLOCK else _pick(k, range(MAX_K_BLOCK - MAX_K_BLOCK % LANES, 0, -LANES))
    return tm, tn, tk


def _mm(x, w, layer, *, n=None, col_off=0, out_dtype=F32, res=None, scale=None, name="mm"):
    m, k = x.shape
    n = w.shape[2] if n is None else n
    tm, tn, tk = _mm_tiles(m, n, k, wide=res is None)
    nk = k // tk
    assert col_off % tn == 0 and m % tm == 0 and n % tn == 0 and k % tk == 0
    cb = col_off // tn
    in_specs = [pl.BlockSpec((tm, tk), lambda i, j, kk: (i, kk)),
                pl.BlockSpec((None, tk, tn), lambda i, j, kk: (layer, kk, j + cb))]
    args = [x, w]
    if res is not None:
        in_specs.append(pl.BlockSpec((tm, tn), lambda i, j, kk: (i, j)))
        args.append(res)
    return pl.pallas_call(
        functools.partial(_mm_kernel, nk=nk, scale=scale, has_res=res is not None),
        out_shape=jax.ShapeDtypeStruct((m, n), out_dtype),
        grid=(m // tm, n // tn, nk),
        in_specs=in_specs,
        out_specs=pl.BlockSpec((tm, tn), lambda i, j, kk: (i, j)),
        scratch_shapes=[pltpu.VMEM((tm, tn), F32)] if nk > 1 else [],
        compiler_params=_cparams("parallel", "parallel", "arbitrary"),
        name=name,
    )(*args)


def _glu_kernel(x_ref, wa_ref, wb_ref, r_ref, o_ref):
    x = x_ref[...]
    a = jnp.dot(x, wa_ref[...], preferred_element_type=F32)
    b = jnp.dot(x, wb_ref[...], preferred_element_type=F32)
    o_ref[...] = r_ref[...] + a * jax.nn.sigmoid(b)


def _glu_mm(x, w, layer, res):
    m, k = x.shape
    d = w.shape[2] // 2
    tm = _pick(m, (1024,))
    tn = _pick(d, (512, 256, 128))
    nj = d // tn
    return pl.pallas_call(
        _glu_kernel,
        out_shape=jax.ShapeDtypeStruct((m, d), F32),
        grid=(m // tm, nj),
        in_specs=[pl.BlockSpec((tm, k), lambda i, j: (i, 0)),
                  pl.BlockSpec((None, k, tn), lambda i, j: (layer, 0, j)),
                  pl.BlockSpec((None, k, tn), lambda i, j: (layer, 0, j + nj)),
                  pl.BlockSpec((tm, tn), lambda i, j: (i, j))],
        out_specs=pl.BlockSpec((tm, tn), lambda i, j: (i, j)),
        compiler_params=_cparams("parallel", "parallel"),
        name="glu_mm",
    )(x, w, w, res)


def _ssm_discretise(ar, ai, log_dt):
    dt = jnp.exp(log_dt)
    mag = jnp.exp(ar * dt)
    return mag * jnp.cos(ai * dt), mag * jnp.sin(ai * dt)


def _ssm_prep_kernel(are_ref, aim_ref, ldt_ref, arc_ref, aic_ref, ldc_ref, btr_ref, bti_ref, ctr_ref, cti_ref,
                     lr_ref, li_ref, wb_ref, wc_ref):
    lr_ref[...], li_ref[...] = _ssm_discretise(are_ref[...], aim_ref[...], ldt_ref[...])
    ar, ai = arc_ref[...], aic_ref[...]
    lr, li = _ssm_discretise(ar, ai, ldc_ref[...])
    den = ar * ar + ai * ai
    nr = lr - 1.0
    kr = (nr * ar + li * ai) / den
    ki = (li * ar - nr * ai) / den
    btr, bti = btr_ref[...], bti_ref[...]
    rows = SUBLANES * SSM_GROUP
    b_re = (kr * btr - ki * bti).reshape(rows, LANES)
    b_im = (kr * bti + ki * btr).reshape(rows, LANES)
    c_re = ctr_ref[...].reshape(rows, LANES)
    c_im = -cti_ref[...].reshape(rows, LANES)
    row_group = lax.broadcasted_iota(jnp.int32, (rows, LANES), 0) // SSM_GROUP
    lane_half = lax.broadcasted_iota(jnp.int32, (rows, LANES), 1) // (LANES // 2)
    nblk = SUBLANES // 2
    for half, (bsrc, csrc) in enumerate(((b_re, c_re), (b_im, c_im))):
        for mblk in range(nblk):
            keep = row_group == 2 * mblk + lane_half
            sl = slice((half * nblk + mblk) * LANES, (half * nblk + mblk + 1) * LANES)
            wb_ref[:, sl] = jnp.where(keep, bsrc, 0.0).astype(wb_ref.dtype)
            wc_ref[:, sl] = jnp.where(keep, csrc, 0.0).astype(wc_ref.dtype)


def _ssm_prep(a_re, a_im, log_dt, b_re, b_im, c_re, c_im):
    g, p = a_re.shape
    assert 2 * p == LANES and g % SUBLANES == 0
    nt = g // SUBLANES
    width = SUBLANES * LANES
    dup = lambda a: jnp.concatenate([a, a], axis=-1)
    btr = dup(jnp.swapaxes(b_re, 1, 2))
    bti = dup(jnp.swapaxes(b_im, 1, 2))
    ldt = jnp.broadcast_to(log_dt[:, None], (g, LANES))
    on_rows = lambda a: jnp.broadcast_to(a[:, None, :], (g, SSM_GROUP, LANES))
    vec =pl.BlockSpec((SUBLANES, LANES), lambda j: (j, 0))
    cube = pl.BlockSpec((SUBLANES, SSM_GROUP, LANES), lambda j: (j, 0, 0))
    wide = pl.BlockSpec((None, SUBLANES * SSM_GROUP, width), lambda j: (j, 0, 0))
    return pl.pallas_call(
        _ssm_prep_kernel,
        out_shape=[jax.ShapeDtypeStruct((g, LANES), F32), jax.ShapeDtypeStruct((g, LANES), F32),
                   jax.ShapeDtypeStruct((nt, SUBLANES * SSM_GROUP, width), BF16),
                   jax.ShapeDtypeStruct((nt, SUBLANES * SSM_GROUP, width), BF16)],
        grid=(nt,),
        in_specs=[vec, vec, vec, cube, cube, cube, cube, cube, cube, cube],
        out_specs=[vec, vec, wide, wide],
        compiler_params=_cparams("parallel"),
        name="ssm_prep",
    )(dup(a_re), dup(a_im), ldt, on_rows(dup(a_re)), on_rows(dup(a_im)), on_rows(ldt),
      btr, bti, dup(c_re), dup(c_im))


SCAN_CHUNK = 128
SCAN_UNROLL = 4


def _sublane_transpose(xs):
    sub = lax.broadcasted_iota(jnp.int32, xs[0].shape, 1)
    for d in (4, 2, 1):
        out = list(xs)
        for i in range(SUBLANES):
            if i & d == 0:
                a, b = xs[i], xs[i | d]
                clear = (sub & d) == 0
                out[i] = jnp.where(clear, a, pltpu.roll(b, d, axis=1))
                out[i | d] = jnp.where(clear, pltpu.roll(a, SUBLANES - d, axis=1), b)
        xs = out
    return xs


def _rows_by_token(blocks):
    n = blocks[0].shape[0] // SUBLANES
    ys = _sublane_transpose([b.reshape(n, SUBLANES, LANES) for b in blocks])
    return jnp.stack(ys, axis=1).reshape(n * SUBLANES * SUBLANES, LANES)


def _rows_by_sequence(x):
    n = x.shape[0] // (SUBLANES * SUBLANES)
    x4 = x.reshape(n, SUBLANES, SUBLANES, LANES)
    xs = _sublane_transpose([x4[:, t] for t in range(SUBLANES)])
    return [b.reshape(n * SUBLANES, LANES) for b in xs]


def _ssm_scan_kernel(*refs, paired, steps):
    if paired:
        u_refs, refs = refs[:SUBLANES], refs[SUBLANES:]
        (wb0_ref, wb1_ref, wc0_ref, wc1_ref, lr_ref, li_ref, d_ref, h0_ref), refs = refs[:8], refs[8:]
        z_refs, (hf_ref, bu_ref, hs_ref) = refs[:SUBLANES], refs[SUBLANES:]
    else:
        (u_ref, wb0_ref, wc0_ref, lr_ref, li_ref, d_ref, h0_ref,
         z_ref, hf_ref, bu_ref, hs_ref) = refs
    half = hs_ref.shape[1] // 2

    @pl.when(pl.program_id(1) == 0)
    def _():
        hs_ref[...] = h0_ref[...]

    if paired:
        u = _rows_by_token([r[...] for r in u_refs])
        odd = lax.broadcasted_iota(jnp.int32, (u.shape[0], 1), 0) % SUBLANES >= SUBLANES // 2
        ub = jnp.concatenate([jnp.where(odd, 0.0, u), jnp.where(odd, u, 0.0)], axis=1).astype(BF16)
        wb = jnp.concatenate([wb0_ref[...], wb1_ref[...]], axis=0)
    else:
        u = u_ref[...]
        ub, wb = u.astype(BF16), wb0_ref[...]
    bu_ref[...] = jnp.dot(ub, wb, preferred_element_type=F32)

    lr, li = lr_ref[...], li_ref[...]

    def step(t, carry):
        hr, hi = carry
        row = pl.multiple_of(t * SUBLANES, SUBLANES)
        blk = bu_ref[pl.ds(row, SUBLANES), :]
        nr = hr * lr - hi * li + blk[:, :half]
        ni = hr * li + hi * lr + blk[:, half:]
        bu_ref[pl.ds(row, SUBLANES), :] = jnp.concatenate([nr, ni], axis=1)
        return nr, ni

    h0 = hs_ref[...]
    hr, hi = lax.fori_loop(0, steps, step, (h0[:, :half], h0[:, half:]), unroll=SCAN_UNROLL)
    hfin = jnp.concatenate([hr, hi], axis=1)
    hs_ref[...] = hfin
    hf_ref[...] = hfin

    hs = bu_ref[...].astype(BF16)
    if paired:
        wc = jnp.concatenate([wc0_ref[...], wc1_ref[...]], axis=0)
        y2 = lax.dot_general(hs, wc, NT_DIMS, preferred_element_type=F32)
        y = jnp.where(odd, y2[:, LANES:], y2[:, :LANES])
    else:
        y = lax.dot_general(hs, wc0_ref[...], NT_DIMS, preferred_element_type=F32)
    y = (y.reshape(steps, SUBLANES, LANES) + d_ref[...][None] * u.reshape(steps, SUBLANES, LANES))
    z = jax.nn.gelu(y).reshape(steps * SUBLANES, LANES)
    if paired:
        for z_ref, zs in zip(z_refs, _rows_by_sequence(z)):
            z_ref[...] = zs.astype(z_ref.dtype)
    else:
        z_ref[...] = z.astype(z_ref.dtype)


def _ssm_scan(u_rows, wb, wc, lr_slab, li_slab, d_slab, h0, *, paired):
    rows, c = u_rows.shape
    width = wb.shape[2]
    off = wb.shape[0] // 2
    t = rows // (SUBLANES // 2 if paired else SUBLANES)
    nj = off if paired else c // LANES
    lt = _pick(t, (SCAN_CHUNK,))
    w_lo = pl.BlockSpec((None, LANES, width), lambda j, s: (j, 0, 0))
    w_hi = pl.BlockSpec((None, LANES, width), lambda j, s: (j + off, 0, 0))
    slab = lambda n: pl.BlockSpec((None, SUBLANES, n), lambda j, s: (j, 0, 0))
    if paired:
        assert t % lt == 0 and lt % SUBLANES == 0
        seq_blocks = t // lt
        u_specs = [pl.BlockSpec((lt, LANES), lambda j, s, b=b, h=h: (b * seq_blocks + s, j + h * off))
                   for h in range(2) for b in range(SUBLANES // 2)]
        in_specs = [*u_specs, w_lo, w_hi, w_lo, w_hi]
        args = [*([u_rows] * SUBLANES), wb, wb, wc, wc]
        z_shapes = [jax.ShapeDtypeStruct((t, c // 2), BF16)] * SUBLANES
        z_specs = [pl.BlockSpec((lt, LANES), lambda j, s: (s, j))] * SUBLANES
    else:
        u_spec = pl.BlockSpec((lt * SUBLANES, LANES), lambda j, s: (s, j))
        in_specs = [u_spec, w_lo, w_lo]
        args = [u_rows, wb, wc]
        z_shapes = [jax.ShapeDtypeStruct((rows, c), BF16)]
        z_specs = [u_spec]
    in_specs += [slab(width // 2), slab(width // 2), slab(LANES), slab(width)]
    args += [lr_slab, li_slab, d_slab, h0]
    return pl.pallas_call(
        functools.partial(_ssm_scan_kernel, paired=paired, steps=lt),
        out_shape=[*z_shapes, jax.ShapeDtypeStruct((nj, SUBLANES, width), F32)],
        grid=(nj, t // lt),
        in_specs=in_specs,
        out_specs=[*z_specs, slab(width)],
        scratch_shapes=[pltpu.VMEM((lt * SUBLANES, width), F32), pltpu.VMEM((SUBLANES, width), F32)],
        compiler_params=_cparams("parallel", "arbitrary"),
        name="ssm_scan",
    )(*args)


def _ssm_tiles(v, nt):
    return v[:, : LANES // 2].reshape(nt, SUBLANES * (LANES // 2))


def _ssm_mixer_prompt(u, prep, d, batch):
    lr, li, wb, wc = prep
    nt = wb.shape[0]
    bt, dm = u.shape
    t = bt // batch
    assert 2 * batch == SUBLANES and nt % 2 == 0
    hw = SUBLANES * (LANES // 2)
    def pair_slab(tiles):
        n = tiles.shape[1]
        s = jnp.stack([tiles[: nt // 2], tiles[nt // 2:]], axis=1)
        return jnp.broadcast_to(s[:, :, None], (nt // 2, 2, batch, n)).reshape(nt // 2, SUBLANES, n)

    *zs, hfin = _ssm_scan(u, wb, wc, pair_slab(_ssm_tiles(lr, nt)), pair_slab(_ssm_tiles(li, nt)),
                          pair_slab(d.reshape(nt, LANES)), jnp.zeros((nt // 2, SUBLANES, 2 * hw), F32),
                          paired=True)
    z = jnp.stack([jnp.concatenate([zs[b], zs[batch + b]], axis=-1) for b in range(batch)]).reshape(bt, dm)

    def states(h):
        h = h.reshape(nt // 2, 2, batch, SUBLANES, LANES // 2).transpose(2, 1, 0, 3, 4)
        return h.reshape(batch, nt * SUBLANES, LANES // 2)

    return z, states(hfin[..., :hw]), states(hfin[..., hw:])


def _ssm_mixer_sample(u, prep, d, h0_re, h0_im):
    lr, li, wb, wc = prep
    nt = wb.shape[0]
    bd = h0_re.shape[0]
    bt, dm = u.shape
    t = bt // bd
    assert bd == SUBLANES
    hw = SUBLANES * (LANES // 2)
    u_rows = u.reshape(bd, t, dm).transpose(1, 0, 2).reshape(t * bd, dm)
    slab = lambda tiles: jnp.broadcast_to(tiles[:, None], (nt, SUBLANES, tiles.shape[1]))
    to_tiles = lambda h: h.reshape(bd, nt, hw).transpose(1, 0, 2)
    h0 = jnp.concatenate([to_tiles(h0_re), to_tiles(h0_im)], axis=-1)
    z_rows, hfin = _ssm_scan(u_rows, wb, wc, slab(_ssm_tiles(lr, nt)), slab(_ssm_tiles(li, nt)),
                             slab(d.reshape(nt, LANES)), h0, paired=False)
    z = z_rows.reshape(t, bd, dm).transpose(1, 0, 2).reshape(bt, dm)
    states = lambda h: h.transpose(1, 0, 2).reshape(bd, nt * SUBLANES, LANES // 2)
    return z, states(hfin[..., :hw]), states(hfin[..., hw:])


def _rel_bucket_np(dist, num_buckets):
    n = np.maximum(dist, 0)
    max_exact = num_buckets // 2
    nf = np.maximum(n, 1).astype(np.float32)
    large = max_exact + (np.log(nf / np.float32(max_exact)) / np.float32(math.log(MAX_DISTANCE / max_exact))
                         * np.float32(num_buckets - max_exact)).astype(np.int32)
    large = np.minimum(large, num_buckets - 1)
    return np.where(n < max_exact, n, large).astype(np.int32)


def _far_distance(num_buckets):
    b = _rel_bucket_np(np.arange(4 * MAX_DISTANCE), num_buckets)
    return int(np.max(np.nonzero(b != num_buckets - 1)[0])) + 1


def _bias_kernel(rb_ref, idx_ref, o_ref, *, num_buckets):
    h = pl.program_id(0)
    idx = idx_ref[...]
    acc = jnp.full(idx.shape, MASK_VALUE, F32)
    for b in range(num_buckets):
        acc = jnp.where(idx == b, rb_ref[b, h], acc)
    o_ref[...] = acc


def _bias_tables(rel_bias, idx):
    nb, nh = rel_bias.shape
    r, c = idx.shape
    return pl.pallas_call(
        functools.partial(_bias_kernel, num_buckets=nb),
        out_shape=jax.ShapeDtypeStruct((nh, r, c), F32),
        grid=(nh,),
        in_specs=[pl.BlockSpec(memory_space=pltpu.SMEM),
                  pl.BlockSpec((r, c), lambda h: (0, 0))],
        out_specs=pl.BlockSpec((None, r, c), lambda h: (h, 0, 0)),
        compiler_params=_cparams("parallel"),
        name="rel_bias_tables",
    )(rel_bias, jnp.asarray(idx))


def _diff_lambda(lam_ref, lam_init):
    lp = lam_ref[...]
    e1 = jnp.exp(jnp.sum(lp[0:1] * lp[1:2], keepdims=True))
    e2 = jnp.exp(jnp.sum(lp[2:3] * lp[3:4], keepdims=True))
    return e1 - e2 + lam_init


def _diff_merge(o1, o2, lam, subln, lam_init):
    d = o1 - lam * o2
    d = d * lax.rsqrt(jnp.mean(d * d, axis=-1, keepdims=True) + SUBLN_EPS) * subln
    return d * (1.0 - lam_init)


ATTN_HEADS_PER_STEP = 4


def _attn_prompt_kernel(lam_ref, subln_ref, q_ref, k_ref, v_ref, bias_ref, o_ref,
                        kb_ref, vt_ref, m_ref, l_ref, acc_ref, *, blk, lam_init):
    qi = pl.program_id(2)
    heads = kb_ref.shape[0]
    e = LANES

    @pl.when(qi == 0)
    def _():
        for h in range(heads):
            for j in range(kb_ref.shape[1]):
                rows, cols = slice(j * blk, (j + 1) * blk), slice(h * e, (h + 1) * e)
                kb_ref[h, j] = k_ref[rows, cols].astype(BF16)
                vt_ref[h, j] = v_ref[rows, cols].T.astype(BF16)

    feat = lax.broadcasted_iota(jnp.int32, (e, blk), 0)
    qts = []
    for h in range(heads):
        qt = q_ref[:, h * e:(h + 1) * e].astype(F32).T
        qts.append(jnp.concatenate([jnp.where(feat < e // 2, qt, 0.0), jnp.where(feat >= e // 2, qt, 0.0)],
                                   axis=1).astype(BF16))

    m_ref[...] = jnp.full(m_ref.shape, MASK_VALUE, F32)
    l_ref[...] = jnp.zeros(l_ref.shape, F32)
    acc_ref[...] = jnp.zeros(acc_ref.shape, F32)

    def attend(kbs, tile):
        sts, ps, corrs = [], [], []
        for h in range(heads):
            keys = jnp.concatenate([kb_ref[h, kb] for kb in kbs], axis=0)
            sts.append(jnp.dot(keys, qts[h], preferred_element_type=F32))
        for h in range(heads):
            if tile is None:
                bias = bias_ref[h, 1, 0:1, blk - 1:blk]
            else:
                bt = bias_ref[h, tile]
                bias = jnp.concatenate([bt, bt], axis=1)
            st = sts[h] + bias
            m_prev = m_ref[h]
            m_new = jnp.maximum(m_prev, jnp.max(st, axis=0, keepdims=True))
            corr = jnp.exp(m_prev - m_new)
            p = jnp.exp(st - m_new)
            l_ref[h] = l_ref[h] * corr + jnp.sum(p, axis=0, keepdims=True)
            m_ref[h] = m_new
            ps.append(p.astype(BF16))
            corrs.append(corr)
        for h in range(heads):
            vals = jnp.concatenate([vt_ref[h, kb] for kb in kbs], axis=1)
            acc_ref[h] = acc_ref[h] * corrs[h] + jnp.dot(vals, ps[h], preferred_element_type=F32)

    n_far = jnp.maximum(qi - 1, 0)

    def far_body(i, carry):
        attend([2 * i, 2 * i + 1], None)
        return carry

    lax.fori_loop(0, n_far // 2, far_body, 0)

    @pl.when(n_far % 2 == 1)
    def _():
        attend([n_far - 1], None)

    @pl.when(qi >= 1)
    def _():
        attend([qi - 1], 1)

    attend([qi], 0)

    lam = _diff_lambda(lam_ref, lam_init)
    for h in range(heads):
        ot = acc_ref[h] / l_ref[h]
        dt = ot[:, :blk] - lam * ot[:, blk:]
        dt = dt * lax.rsqrt(jnp.mean(dt * dt, axis=0, keepdims=True) + SUBLN_EPS)
        o_ref[:, h * e:(h + 1) * e] = (dt.T * subln_ref[...] * (1.0 - lam_init)).astype(o_ref.dtype)


def _attn_prompt(q, k, v, bias, lam_p, subln, *, batch, lam_init, blk):
    bs, he = q.shape
    s = bs // batch
    e = LANES
    nh = he // e
    nq = s // blk
    hps = ATTN_HEADS_PER_STEP if nh % ATTN_HEADS_PER_STEP == 0 else 1
    w = hps * e
    return pl.pallas_call(
        functools.partial(_attn_prompt_kernel, blk=blk, lam_init=lam_init),
        out_shape=jax.ShapeDtypeStruct((bs, he), BF16),
        grid=(batch, nh // hps, nq),
        in_specs=[pl.BlockSpec(lam_p.shape, lambda b, h, i: (0, 0)),
                  pl.BlockSpec((1, e), lambda b, h, i: (0, 0)),
                  pl.BlockSpec((blk, w), lambda b, h, i: (b * nq + i, h)),
                  pl.BlockSpec((s, w), lambda b, h, i: (b, h)),
                  pl.BlockSpec((s, w), lambda b, h, i: (b, h)),
                  pl.BlockSpec((hps, 2, blk, blk), lambda b, h, i: (h, 0, 0, 0))],
        out_specs=pl.BlockSpec((blk, w), lambda b, h, i: (b * nq + i, h)),
        scratch_shapes=[pltpu.VMEM((hps, nq, blk, e), BF16), pltpu.VMEM((hps, nq, e, blk), BF16),
                        pltpu.VMEM((hps, 1, 2 * blk), F32), pltpu.VMEM((hps, 1, 2 * blk), F32),
                        pltpu.VMEM((hps, e, 2 * blk), F32)],
        compiler_params=_cparams("parallel", "parallel", "arbitrary"),
        name="diff_attn_prompt",
    )(lam_p, subln.reshape(1, e), q, k, v, bias)


HEAD_GROUP = SUBLANES
DECODE_PAGES_PER_STEP = 4


def _attn_sample_kernel(pt_ref, lam_ref, subln_ref, q_ref, *refs, n_steps, pps, lam_init):
    ck_refs, cv_refs = refs[:pps], refs[pps:2 * pps]
    kn_ref, vn_ref, bias_far_ref, bias_ref, biasn_ref, o_ref, m_ref, l_ref, acc_ref = refs[2 * pps:]
    p = pl.program_id(1)
    ng = q_ref.shape[0]

    @pl.when(p == 0)
    def _():
        m_ref[...] = jnp.full(m_ref.shape, MASK_VALUE, F32)
        l_ref[...] = jnp.zeros(l_ref.shape, F32)
        acc_ref[...] = jnp.zeros(acc_ref.shape, F32)

    def attend(k_ref, v_ref, b_ref):
        keys = k_ref.shape[0]
        group = lambda ref, g: ref[:, g * HEAD_GROUP:(g + 1) * HEAD_GROUP, :].reshape(
            keys * HEAD_GROUP, LANES).astype(BF16)
        scores = [lax.dot_general(q_ref[g], group(k_ref, g), NT_DIMS, preferred_element_type=F32)
                  for g in range(ng)]
        probs, corrs = [], []
        for g in range(ng):
            s = scores[g] + b_ref[g]
            m_prev = m_ref[g]
            m_new = jnp.maximum(m_prev, jnp.max(s, axis=-1, keepdims=True))
            corr = jnp.exp(m_prev - m_new)
            pe = jnp.exp(s - m_new)
            l_ref[g] = l_ref[g] * corr + jnp.sum(pe, axis=-1, keepdims=True)
            m_ref[g] = m_new
            probs.append(pe.astype(BF16))
            corrs.append(corr)
        for g in range(ng):
            acc_ref[g] = acc_ref[g] * corrs[g] + jnp.dot(probs[g], group(v_ref, g), preferred_element_type=F32)

    @pl.when(p < n_steps)
    def _():
        for r in range(pps):
            attend(ck_refs[r], cv_refs[r], bias_ref if r == pps - 1 else bias_far_ref)

    @pl.when(p == n_steps)
    def _():
        attend(kn_ref, vn_ref, biasn_ref)
        o = acc_ref[...] / l_ref[...]
        half = o.shape[1] // 2
        lam = _diff_lambda(lam_ref, lam_init)
        o_ref[...] = _diff_merge(o[:, :half], o[:, half:], lam, subln_ref[...], lam_init).astype(o_ref.dtype)


def _attn_sample(q, k_new, v_new, cache_k, cache_v, layer, page_table, tabs, lam_p, subln, *, lam_init):
    bd, n_pages = page_table.shape
    page, nh, e = cache_k.shape[2:]
    t = q.shape[0] // bd
    assert e == LANES and nh % HEAD_GROUP == 0 and t <= SUBLANES
    ng = nh // HEAD_GROUP
    hh = e // 2
    rows = 2 * HEAD_GROUP * t

    q5 = q.reshape(bd, t, ng, HEAD_GROUP, e).transpose(0, 2, 3, 1, 4)
    lane = jnp.arange(e) < hh
    qm = jnp.stack([jnp.where(lane, q5, 0), jnp.where(lane, 0, q5)], axis=2)
    qm = qm.reshape(bd, ng, rows, e)

    same = np.arange(HEAD_GROUP)[:, None] == np.arange(HEAD_GROUP)[None, :]

    def expand(tab, keys):
        tb = tab[:, :t, :keys].reshape(ng, HEAD_GROUP, t, keys)
        full = jnp.where(same[None, :, None, None, :], tb[..., None], MASK_VALUE)
        full = jnp.broadcast_to(full[:, None], (ng, 2, HEAD_GROUP, t, keys, HEAD_GROUP))
        return full.reshape(ng, rows, keys * HEAD_GROUP)

    bias_pages = jnp.stack([expand(tabs[:, 0], page), expand(tabs[:, 1], page)])
    bias_new = expand(tabs[:, 2], SUBLANES)

    pad = lambda a: jnp.pad(a.reshape(bd, t, nh, e), ((0, 0), (0, SUBLANES - t), (0, 0), (0, 0)))
    pps = _pick(n_pages, (DECODE_PAGES_PER_STEP, 2, 1))
    n_steps = n_pages // pps
    page_maps = [lambda b, p, pt, r=r: (layer, pt[b, jnp.minimum(p, n_steps - 1) * pps + r], 0, 0, 0)
                 for r in range(pps)]
    page_specs = [pl.BlockSpec((None, None, page, nh, e), pm) for pm in page_maps]
    const2 = lambda b, p, pt: (0, 0)
    per_b = lambda b, p, pt: (b, 0, 0, 0)
    bias_spec = lambda sel: pl.BlockSpec((None, ng, rows, page * HEAD_GROUP), lambda b, p, pt: (sel(p), 0, 0, 0))
    out = pl.pallas_call(
        functools.partial(_attn_sample_kernel, n_steps=n_steps, pps=pps, lam_init=lam_init),
        out_shape=jax.ShapeDtypeStruct((bd, ng, rows // 2, e), BF16),
        grid_spec=pltpu.PrefetchScalarGridSpec(
            num_scalar_prefetch=1,
            grid=(bd, n_steps + 1),
            in_specs=[pl.BlockSpec(lam_p.shape, const2),
                      pl.BlockSpec((1, e), const2),
                      pl.BlockSpec((None, ng, rows, e), per_b),
                      *page_specs, *page_specs,
                      pl.BlockSpec((None, SUBLANES, nh, e), per_b),
                      pl.BlockSpec((None, SUBLANES, nh, e), per_b),
                      bias_spec(lambda p: 0),
                      bias_spec(lambda p: jnp.where(p >= n_steps - 1, 1, 0)),
                      pl.BlockSpec((ng, rows, SUBLANES * HEAD_GROUP), lambda b, p, pt: (0, 0, 0))],
            out_specs=pl.BlockSpec((None, ng, rows // 2, e), per_b),
            scratch_shapes=[pltpu.VMEM((ng, rows, 1), F32), pltpu.VMEM((ng, rows, 1), F32),
                            pltpu.VMEM((ng, rows, e), F32)]),
        compiler_params=_cparams("parallel", "arbitrary"),
        name="diff_attn_sample",
    )(page_table, lam_p, subln.reshape(1, e), qm, *([cache_k] * pps), *([cache_v] * pps),
      pad(k_new), pad(v_new), bias_pages, bias_pages, bias_new)
    return out.reshape(bd, ng, HEAD_GROUP, t, e).transpose(0, 3, 1, 2, 4).reshape(bd * t, nh * e)


def _cross_kernel(q_ref, k_ref, v_ref, o_ref, *, heads, scale):
    q = q_ref[...]
    k = k_ref[...].astype(BF16)
    v = v_ref[...].astype(BF16)
    outs = []
    for h in range(heads):
        sl = slice(h * LANES, (h + 1) * LANES)
        s = lax.dot_general(q[:, sl], k[:, sl], NT_DIMS, preferred_element_type=F32) * scale
        e = jnp.exp(s - jnp.max(s, axis=-1, keepdims=True))
        o = jnp.dot(e.astype(BF16), v[:, sl], preferred_element_type=F32)
        outs.append(o / jnp.sum(e, axis=-1, keepdims=True))
    o_ref[...] = jnp.concatenate(outs, axis=1).astype(o_ref.dtype)


def _cross_attn(q, mk, mv, *, batch, shared_q):
    rows, w = q.shape
    nm = mk.shape[0] // batch
    heads = w // LANES
    if shared_q:
        tq, nq = rows, 1
        q_map = lambda b, i: (0, 0)
    else:
        per = rows // batch
        tq = _pick(per, (512, 256, 128))
        nq = per // tq
        q_map = lambda b, i: (b * nq + i, 0)
    return pl.pallas_call(
        functools.partial(_cross_kernel, heads=heads, scale=LANES ** -0.5),
        out_shape=jax.ShapeDtypeStruct((batch * nq * tq, w), BF16),
        grid=(batch, nq),
        in_specs=[pl.BlockSpec((tq, w), q_map),
                  pl.BlockSpec((nm, w), lambda b, i: (b, 0)),
                  pl.BlockSpec((nm, w), lambda b, i: (b, 0))],
        out_specs=pl.BlockSpec((tq, w), lambda b, i: (b * nq + i, 0)),
        compiler_params=_cparams("parallel", "parallel"),
        name="cross_attn",
    )(q, mk, mv)


FFN_TILE = 256
FFN_SUB_ROWS = 256


def _ffn_sub_tiles(tm):
    sub = min(tm, FFN_SUB_ROWS)
    return [slice(a, a + sub) for a in range(0, tm, sub)]


def _silu_gate(cg, cv):
    return jax.nn.silu(cg) * cv


def _ffn_up_kernel(x_ref, w_ref, bg_ref, bv_ref, cwg_ref, cwv_ref, cbg_ref, cbv_ref,
                   h_ref, sg_ref, sv_ref, carry_g, carry_v, *, tiles_per_seq):
    i, j = pl.program_id(0), pl.program_id(1)
    tm, tn = h_ref.shape

    @pl.when(i % tiles_per_seq == 0)
    def _():
        carry_g[j, SUBLANES - 2:, :] = bg_ref[...]
        carry_v[j, SUBLANES - 2:, :] = bv_ref[...]

    def conv(u, prev, cw_ref, cb_ref):
        rows = lax.broadcasted_iota(jnp.int32, u.shape, 0)
        p1, p2 = prev[SUBLANES - 1:], prev[SUBLANES - 2:SUBLANES - 1]
        s1 = jnp.where(rows == 0, p1, pltpu.roll(u, 1, axis=0))
        s2 = jnp.where(rows == 0, p2, jnp.where(rows == 1, p1, pltpu.roll(u, 2, axis=0)))
        cw = cw_ref[...]
        return cb_ref[...] + cw[0:1] * s2 + cw[1:2] * s1 + cw[2:3] * u, u[u.shape[0] - SUBLANES:]

    prev_g, prev_v = carry_g[j], carry_v[j]
    for rs in _ffn_sub_tiles(tm):
        up = jnp.dot(x_ref[rs, :], w_ref[...], preferred_element_type=F32)
        cg, prev_g = conv(up[:, :tn], prev_g, cwg_ref, cbg_ref)
        cv, prev_v = conv(up[:, tn:], prev_v, cwv_ref, cbv_ref)
        h_ref[rs, :] = _silu_gate(cg, cv).astype(h_ref.dtype)
    carry_g[j], carry_v[j] = prev_g, prev_v
    sg_ref[...], sv_ref[...] = prev_g, prev_v


def _ffn_weight_kernel(g_ref, v_ref, o_ref):
    o_ref[:, :FFN_TILE] = g_ref[...].astype(o_ref.dtype)
    o_ref[:, FFN_TILE:] = v_ref[...].astype(o_ref.dtype)


def _interleave_ffn_weight(w):
    nl, k, f2 = w.shape
    nv = f2 // (2 * FFN_TILE)
    tk = _pick(k, (2048, 1024, 512, 256, 128))
    return pl.pallas_call(
        _ffn_weight_kernel,
        out_shape=jax.ShapeDtypeStruct((nl, k, f2), BF16),
        grid=(nl, k // tk, nv),
        in_specs=[pl.BlockSpec((None, tk, FFN_TILE), lambda a, i, j: (a, i, j)),
                  pl.BlockSpec((None, tk, FFN_TILE), lambda a, i, j: (a, i, j + nv))],
        out_specs=pl.BlockSpec((None, tk, 2 * FFN_TILE), lambda a, i, j: (a, i, j)),
        compiler_params=_cparams("parallel", "parallel", "parallel"),
        name="ffn_weight_interleave",
    )(w, w)


def _deinterleave_ffn_tiles(a):
    lead, f2 = a.shape[:-1], a.shape[-1]
    a = a.reshape(*lead, f2 // (2 * FFN_TILE), 2, FFN_TILE)
    return jnp.swapaxes(a, -3, -2).reshape(*lead, f2)


def _ffn_up_prompt(x, w_up_il, layer, buf, conv_w, conv_b, *, batch):
    bt, k = x.shape
    f2 = w_up_il.shape[2]
    f = f2 // 2
    t = bt // batch
    tm = _pick(t, (1024, 512, 256, 128))
    tps = t // tm
    nv = f // FFN_TILE
    half = lambda blk, fn: (pl.BlockSpec(blk, lambda i, j: fn(i, j, 0)), pl.BlockSpec(blk, lambda i, j: fn(i, j, nv)))
    w_spec = pl.BlockSpec((None, k, 2 * FFN_TILE), lambda i, j: (layer, 0, j))
    b_specs = half((None, CONV_WIDTH - 1, FFN_TILE), lambda i, j, o: (i // tps, 0, j + o))
    cw_specs = half((CONV_WIDTH, FFN_TILE), lambda i, j, o: (0, j + o))
    cb_specs = half((1, FFN_TILE), lambda i, j, o: (0, j + o))
    state_spec = pl.BlockSpec((None, SUBLANES, FFN_TILE), lambda i, j: (i, 0, j))
    h, tails_g, tails_v = pl.pallas_call(
        functools.partial(_ffn_up_kernel, tiles_per_seq=tps),
        out_shape=[jax.ShapeDtypeStruct((bt, f), BF16),
                   jax.ShapeDtypeStruct((bt // tm, SUBLANES, f), F32),
                   jax.ShapeDtypeStruct((bt // tm, SUBLANES, f), F32)],
        grid=(bt // tm, nv),
        in_specs=[pl.BlockSpec((tm, k), lambda i, j: (i, 0)), w_spec, *b_specs, *cw_specs, *cb_specs],
        out_specs=[pl.BlockSpec((tm, FFN_TILE), lambda i, j: (i, j)), state_spec, state_spec],
        scratch_shapes=[pltpu.VMEM((nv, SUBLANES, FFN_TILE), F32), pltpu.VMEM((nv, SUBLANES, FFN_TILE), F32)],
        compiler_params=_cparams("arbitrary", "arbitrary"),
        name="ffn_up_conv_p",
    )(x, w_up_il, buf, buf, conv_w, conv_w, conv_b.reshape(1, f2), conv_b.reshape(1, f2))
    return h, tails_g[tps - 1::tps], tails_v[tps - 1::tps]


def _conv_sample_kernel(x2g, x1g, x0g, x2v, x1v, x0v, wg_ref, wv_ref, cbg_ref, cbv_ref, o_ref):
    wg, wv = wg_ref[...], wv_ref[...]
    cg = cbg_ref[...] + wg[0:1] * x2g[...] + wg[1:2] * x1g[...] + wg[2:3] * x0g[...]
    cv = cbv_ref[...] + wv[0:1] * x2v[...] + wv[1:2] * x1v[...] + wv[2:3] * x0v[...]
    o_ref[...] = _silu_gate(cg, cv).astype(o_ref.dtype)


def _conv_gate_sample(hp, conv_w, conv_b):
    bd, t2, f2 = hp.shape
    t = t2 - (CONV_WIDTH - 1)
    f = f2 // 2
    nv = f // FFN_TILE
    shifted = [hp[:, s:s + t].reshape(bd * t, f2) for s in range(CONV_WIDTH)]
    xs = lambda off: pl.BlockSpec((bd * t, FFN_TILE), lambda j: (0, j + off))
    taps = lambda off: pl.BlockSpec((CONV_WIDTH, FFN_TILE), lambda j: (0, j + off))
    cb = lambda off: pl.BlockSpec((1, FFN_TILE), lambda j: (0, j + off))
    return pl.pallas_call(
        _conv_sample_kernel,
        out_shape=jax.ShapeDtypeStruct((bd * t, f), BF16),
        grid=(nv,),
        in_specs=[xs(0)] * 3 + [xs(nv)] * 3 + [taps(0), taps(nv), cb(0), cb(nv)],
        out_specs=pl.BlockSpec((bd * t, FFN_TILE), lambda j: (0, j)),
        compiler_params=_cparams("parallel"),
        name="conv_gate_sample",
    )(*shifted, *shifted, conv_w, conv_w, conv_b.reshape(1, f2), conv_b.reshape(1, f2))


def kernel(x_prompt, x_sample, cache_attn_k, cache_attn_v, cache_mem_k, cache_mem_v, state_ssm_re, state_ssm_im, state_ffn_conv, page_table, mem_prompt, norm_mix, norm_cross, norm_ffn, norm_final, ssm_w_in, ssm_a_re, ssm_a_im, ssm_log_dt, ssm_b_re, ssm_b_im, ssm_c_re, ssm_c_im, ssm_d, ssm_w_glu, attn_w_qkv, attn_lambda, attn_subln, attn_w_o, rel_bias, cross_w_q, cross_w_kv, cross_w_o, ffn_w_up, ffn_conv_w, ffn_conv_b, ffn_w_down):
    bp, seq, dm = x_prompt.shape
    bd, dec_t, _ = x_sample.shape
    depth = norm_mix.shape[0]
    page, nh, hv = cache_attn_k.shape[2:]
    n_pages = page_table.shape[1]
    n_mem, mem_heads, mem_hd = cache_mem_k.shape[2:]
    mw = mem_heads * mem_hd
    d_ff = ffn_w_down.shape[1]
    nb = rel_bias.shape[0]
    assert hv == LANES and mem_hd == LANES and d_ff % FFN_TILE == 0
    attn_scale = (hv // 2) ** -0.5

    xp = x_prompt.reshape(bp * seq, dm)
    xs = x_sample.reshape(bd * dec_t, dm)
    mem_b = mem_prompt.reshape(bp * n_mem, dm).astype(BF16)

    blk = min(256, seq)
    far = _far_distance(nb)
    assert blk + 1 >= far and seq % blk == 0
    kpos, qpos = np.arange(blk)[:, None], np.arange(blk)[None, :]
    diag = np.where(kpos > qpos, -1, _rel_bucket_np(qpos - kpos, nb))
    idx_prompt = np.concatenate([diag, _rel_bucket_np(blk + qpos - kpos, nb)], axis=0)
    past = n_pages * page
    assert page + 1 >= far
    tt =np.arange(SUBLANES)[:, None]
    kk = np.arange(page)[None, :]
    tab_far = np.full((SUBLANES, page), nb - 1)
    tab_last = _rel_bucket_np(past + tt - ((n_pages - 1) * page + kk), nb)
    tab_new = np.where((kk <= tt) & (kk < dec_t) & (tt < dec_t), _rel_bucket_np(tt - kk, nb), -1)
    idx_sample = np.concatenate([tab_far, tab_last, tab_new], axis=0).astype(np.int32)

    w_in, w_glu = ssm_w_in.astype(BF16), ssm_w_glu.astype(BF16)
    w_qkv, w_o = attn_w_qkv.astype(BF16), attn_w_o.astype(BF16)
    w_q, w_kv, w_co = cross_w_q.astype(BF16), cross_w_kv.astype(BF16), cross_w_o.astype(BF16)
    w_up, w_down = _interleave_ffn_weight(ffn_w_up), ffn_w_down.astype(BF16)

    outs = {k: [] for k in ("ssm_re_p", "ssm_im_p", "k_p", "v_p", "mk_p", "mv_p", "conv_p",
                            "ssm_re_s", "ssm_im_s", "k_s", "v_s", "conv_s")}
    for i in range(depth):
        j = i // 2
        hp = _rmsnorm(xp, norm_mix[i], BF16)
        hs = _rmsnorm(xs, norm_mix[i], BF16)
        if i % 2 == 0:
            prep = _ssm_prep(ssm_a_re[j], ssm_a_im[j], ssm_log_dt[j], ssm_b_re[j], ssm_b_im[j],
                             ssm_c_re[j], ssm_c_im[j])
            up_ = _mm(hp, w_in, j, name="ssm_in_p")
            us_ = _mm(hs, w_in, j, name="ssm_in_s")
            zp, hrp, hip = _ssm_mixer_prompt(up_, prep, ssm_d[j], bp)
            zs, hrs, his = _ssm_mixer_sample(us_, prep, ssm_d[j], state_ssm_re[j], state_ssm_im[j])
            xp = _glu_mm(zp, w_glu, j, xp)
            xs = _glu_mm(zs, w_glu, j, xs)
            outs["ssm_re_p"].append(hrp)
            outs["ssm_im_p"].append(hip)
            outs["ssm_re_s"].append(hrs)
            outs["ssm_im_s"].append(his)
        else:
            lam_init = 0.8 - 0.6 * math.exp(-0.3 * i)
            width = nh * hv
            qkv = lambda h, tag: (_mm(h, w_qkv, j, n=width, out_dtype=BF16, scale=attn_scale, name="q_" + tag),
                                  _mm(h, w_qkv, j, n=width, col_off=width, name="k_" + tag),
                                  _mm(h, w_qkv, j, n=width, col_off=2 * width, name="v_" + tag))
            qp, kp_, vp_ = qkv(hp, "p")
            qs, ks_, vs_ = qkv(hs, "s")
            bias_p = _bias_tables(rel_bias, idx_prompt).reshape(nh, 2, blk, blk)
            tabs_s = _bias_tables(rel_bias, idx_sample).reshape(nh, 3, SUBLANES, page)
            ap = _attn_prompt(qp, kp_, vp_, bias_p, attn_lambda[j], attn_subln[j],
                              batch=bp, lam_init=lam_init, blk=blk)
            as_ = _attn_sample(qs, ks_, vs_, cache_attn_k, cache_attn_v, j, page_table, tabs_s,
                               attn_lambda[j], attn_subln[j], lam_init=lam_init)
            xp = _mm(ap, w_o, j, res=xp, name="attn_o_p")
            xs = _mm(as_, w_o, j, res=xs, name="attn_o_s")
            outs["k_p"].append(kp_.reshape(bp, seq, nh, hv))
            outs["v_p"].append(vp_.reshape(bp, seq, nh, hv))
            outs["k_s"].append(ks_.reshape(bd, dec_t, nh, hv))
            outs["v_s"].append(vs_.reshape(bd, dec_t, nh, hv))

        hp = _rmsnorm(xp, norm_cross[i], BF16)
        hs = _rmsnorm(xs, norm_cross[i], BF16)
        mkp = _mm(mem_b, w_kv, i, n=mw, name="mem_k")
        mvp = _mm(mem_b, w_kv, i, n=mw, col_off=mw, name="mem_v")
        outs["mk_p"].append(mkp.reshape(bp, n_mem, mem_heads, mem_hd))
        outs["mv_p"].append(mvp.reshape(bp, n_mem, mem_heads, mem_hd))
        cq_p = _mm(hp, w_q, i, out_dtype=BF16, name="cross_q_p")
        cq_s = _mm(hs, w_q, i, out_dtype=BF16, name="cross_q_s")
        co_p = _cross_attn(cq_p, mkp, mvp, batch=bp, shared_q=False)
        co_all = _cross_attn(cq_s, cache_mem_k[i].reshape(bd * n_mem, mw), cache_mem_v[i].reshape(bd * n_mem, mw),
                             batch=bd, shared_q=True)
        co_all = co_all.reshape(bd, bd, dec_t, mw)
        co_s = jnp.stack([co_all[b, b] for b in range(bd)]).reshape(bd * dec_t, mw)
        xp = _mm(co_p, w_co, i, res=xp, name="cross_o_p")
        xs = _mm(co_s, w_co, i, res=xs, name="cross_o_s")

        hp = _rmsnorm(xp, norm_ffn[i], BF16)
        hs = _rmsnorm(xs, norm_ffn[i], BF16)
        buf0 = jnp.zeros((bp, CONV_WIDTH - 1, 2 * d_ff), F32)
        gp, tail_g, tail_v = _ffn_up_prompt(hp, w_up, i, buf0, ffn_conv_w[i], ffn_conv_b[i], batch=bp)
        up_s = _deinterleave_ffn_tiles(_mm(hs, w_up, i, name="ffn_up_s"))
        hp_s = jnp.concatenate([state_ffn_conv[i], up_s.reshape(bd, dec_t, 2 * d_ff)], axis=1)
        gs = _conv_gate_sample(hp_s, ffn_conv_w[i], ffn_conv_b[i])
        xp = _mm(gp, w_down, i, res=xp, name="ffn_down_p")
        xs = _mm(gs, w_down, i, res=xs, name="ffn_down_s")
        keep = CONV_WIDTH - 1
        assert seq >= keep
        outs["conv_p"].append(jnp.concatenate([tail_g[:, -keep:], tail_v[:, -keep:]], axis=-1))
        outs["conv_s"].append(hp_s[:, -keep:])

    y_prompt = _rmsnorm(xp, norm_final, F32).reshape(bp, seq, dm)
    y_sample = _rmsnorm(xs, norm_final, F32).reshape(bd, dec_t, dm)
    st = lambda k: jnp.stack(outs[k])
    return (y_prompt, y_sample, st("ssm_re_p"), st("ssm_im_p"), st("k_p"), st("v_p"),
            st("mk_p"), st("mv_p"), st("conv_p"), st("ssm_re_s"), st("ssm_im_s"),
            st("k_s"), st("v_s"), st("conv_s"))
```
